```python
import jax, jax.numpy as jnp
from jax import lax
import numpy as np

D_MODEL = 4096
BATCH = 4
SEQ = 4096
DEPTH = 1

D_A = D_MODEL // 2
HA_DK = 128
H_A = D_A // HA_DK
HA_DV = D_A // H_A
D_B = D_MODEL // 2
H_B = 8
DV_B = D_B // H_B
DK_B = DV_B // 2
D_QK_B = H_B * DK_B
CONV_W = 4
CHUNK = 64
EPS = 1e-6

COLS = (D_A, D_A, D_A, D_A, D_A,
        D_QK_B, D_QK_B, D_B, D_B, D_B, H_B, H_B,
        D_MODEL, D_MODEL)
D_IN = sum(COLS)
SPLIT_IDX = tuple(int(v) for v in np.cumsum(COLS)[:-1])

kernel_name = "hgrn2_mlstm_gated_parallel_hybrid"


def rmsnorm(x, g):
    xf = x.astype(jnp.float32)
    r = lax.rsqrt(jnp.mean(xf * xf, axis=-1, keepdims=True) + EPS)
    return (xf * r).astype(x.dtype) * g


def head_rmsnorm(o, g, n_heads):
    B, S, W = o.shape
    of = o.astype(jnp.float32).reshape(B, S, n_heads, W // n_heads)
    of = of * lax.rsqrt(jnp.mean(of * of, axis=-1, keepdims=True) + EPS)
    return of.reshape(B, S, W).astype(o.dtype) * g


def to_chunks(a, n_heads, d):
    B, S, _ = a.shape
    return a.reshape(B, S // CHUNK, CHUNK, n_heads, d).transpose(0, 3, 1, 2, 4)


def from_chunks(a):
    B, H, N, C, d = a.shape
    return a.transpose(0, 2, 3, 1, 4).reshape(B, N * C, H * d)


def causal_conv(u, w, b):
    S = u.shape[1]
    up = jnp.pad(u, ((0, 0), (CONV_W - 1, 0), (0, 0)))
    y = b
    for k in range(CONV_W):
        y = y + w[k] * up[:, k:k + S]
    return y


MASK = np.tril(np.ones((CHUNK, CHUNK), dtype=bool))


def hgrn2_mixer(q, f_pre, i, lb):
    dt = q.dtype
    q, f_pre, i, lb = (a.astype(jnp.float32) for a in (q, f_pre, i, lb))
    f = lb + (1.0 - lb) * jax.nn.sigmoid(f_pre)
    logf = jnp.log(f)
    k = 1.0 - f
    qc, kc, vc, lfc = (to_chunks(a, H_A, HA_DK if a is not i else HA_DV) for a in (q, k, i, logf))
    b = jnp.cumsum(lfc, axis=-2)
    b_mid = b[..., CHUNK // 2 - 1:CHUNK // 2, :]
    q_rel = qc * jnp.exp(b - b_mid)
    k_rel = kc * jnp.exp(b_mid - b)
    scores = jnp.einsum('bhncd,bhnsd->bhncs', q_rel, k_rel)
    scores = jnp.where(MASK, scores, 0.0)
    o_intra = jnp.einsum('bhncs,bhnsv->bhncv', scores, vc)
    g = b[..., -1, :]
    q_in = qc * jnp.exp(b)
    k_out = kc * jnp.exp(g[..., None, :] - b)
    U = jnp.einsum('bhncd,bhncv->bhndv', k_out, vc)

    def step(S, inp):
        q_in_j, g_j, U_j = inp
        o = jnp.einsum('bhcd,bhdv->bhcv', q_in_j, S)
        S = jnp.exp(g_j)[..., None] * S + U_j
        return S, o

    B = q.shape[0]
    S0 = jnp.zeros((B, H_A, HA_DK, HA_DV), jnp.float32)
    xs = (jnp.moveaxis(q_in, 2, 0), jnp.moveaxis(g, 2, 0), jnp.moveaxis(U, 2, 0))
    _, o_inter = lax.scan(step, S0, xs)
    o = o_intra + jnp.moveaxis(o_inter, 0, 2)
    return from_chunks(o).astype(dt)


def mlstm_mixer(q, k, v, i_pre, f_pre):
    dt = v.dtype
    q, k, v, i_pre, f_pre = (a.astype(jnp.float32) for a in (q, k, v, i_pre, f_pre))
    B, S, _ = v.shape
    N = S // CHUNK
    qc = to_chunks(q, H_B, DK_B)
    kc = to_chunks(k, H_B, DK_B) * (DK_B ** -0.5)
    vc = to_chunks(v, H_B, DV_B)
    logf = jax.nn.log_sigmoid(f_pre).reshape(B, N, CHUNK, H_B).transpose(0, 3, 1, 2)
    ig = i_pre.reshape(B, N, CHUNK, H_B).transpose(0, 3, 1, 2)
    b = jnp.cumsum(logf, axis=-1)

    def step(carry, inp):
        Cs, ns, m = carry
        q_j, k_j, v_j, b_j, ig_j = inp
        g_j = b_j[..., -1]
        logD = b_j[..., :, None] - b_j[..., None, :] + ig_j[..., None, :]
        logD = jnp.where(MASK, logD, -jnp.inf)
        m_t = jnp.maximum(b_j + m[..., None], jnp.max(logD, axis=-1))
        inter = jnp.exp(b_j + m[..., None] - m_t)
        Dw = jnp.exp(logD - m_t[..., None]) * jnp.einsum('bhcd,bhsd->bhcs', q_j, k_j)
        num = inter[..., None] * jnp.einsum('bhcd,bhdv->bhcv', q_j, Cs) \
            + jnp.einsum('bhcs,bhsv->bhcv', Dw, v_j)
        den = inter * jnp.einsum('bhcd,bhd->bhc', q_j, ns) + jnp.sum(Dw, axis=-1)
        h = num / jnp.maximum(jnp.abs(den), jnp.exp(-m_t))[..., None]
        log_w = g_j[..., None] - b_j + ig_j
        m_new = jnp.maximum(g_j + m, jnp.max(log_w, axis=-1))
        decay = jnp.exp(g_j + m - m_new)
        w = jnp.exp(log_w - m_new[..., None])
        Cs = decay[..., None, None] * Cs + jnp.einsum('bhc,bhcd,bhcv->bhdv', w, k_j, v_j)
        ns = decay[..., None] * ns + jnp.einsum('bhc,bhcd->bhd', w, k_j)
        return (Cs, ns, m_new), h

    carry0 = (jnp.zeros((B, H_B, DK_B, DV_B), jnp.float32),
              jnp.zeros((B, H_B, DK_B), jnp.float32),
              jnp.zeros((B, H_B), jnp.float32))
    xs = tuple(jnp.moveaxis(a, 2, 0) for a in (qc, kc, vc, b, ig))
    _, h = lax.scan(step, carry0, xs)
    return from_chunks(jnp.moveaxis(h, 0, 2)).astype(dt)


def setup_inputs(seed: int = 0) -> dict:
    key = jax.random.key(seed)
    ks = jax.random.split(key, 16)
    f32 = jnp.float32
    nrm = lambda k, shape, s: jax.random.normal(k, shape, f32) * s
    x = nrm(ks[0], (BATCH, SEQ, D_MODEL), 1.0)
    g_pre = 1.0 + nrm(ks[1], (DEPTH, D_MODEL), 0.02)
    w_in = nrm(ks[2], (DEPTH, D_MODEL, D_IN), D_MODEL ** -0.5)
    lb_logits = nrm(ks[3], (DEPTH + 1, D_A), 0.5)
    conv_w = nrm(ks[4], (DEPTH, CONV_W, 2 * D_QK_B), CONV_W ** -0.5)
    conv_b = nrm(ks[5], (DEPTH, 2 * D_QK_B), 0.02)
    b_ig = nrm(ks[6], (DEPTH, H_B), 0.1)
    b_fg = jnp.linspace(3.0, 6.0, H_B, dtype=f32)[None, :] + nrm(ks[7], (DEPTH, H_B), 0.1)
    g_norm_a = 1.0 + nrm(ks[8], (DEPTH, D_A), 0.02)
    g_norm_b = 1.0 + nrm(ks[9], (DEPTH, D_B), 0.02)
    w_up_a = nrm(ks[10], (DEPTH, D_A, D_MODEL), D_A ** -0.5)
    w_up_b = nrm(ks[11], (DEPTH, D_B, D_MODEL), D_B ** -0.5)
    w_out = nrm(ks[12], (DEPTH, D_MODEL, D_MODEL), D_MODEL ** -0.5)
    g_post = 1.0 + nrm(ks[13], (DEPTH, D_MODEL), 0.02)
    return {"x": x, "g_pre": g_pre, "w_in": w_in, "lb_logits": lb_logits,
            "conv_w": conv_w, "conv_b": conv_b, "b_ig": b_ig, "b_fg": b_fg,
            "g_norm_a": g_norm_a, "g_norm_b": g_norm_b, "w_up_a": w_up_a,
            "w_up_b": w_up_b, "w_out": w_out, "g_post": g_post}


def reference(x, g_pre, w_in, lb_logits, conv_w, conv_b, b_ig, b_fg,
              g_norm_a, g_norm_b, w_up_a, w_up_b, w_out, g_post):
    lb_all = jnp.cumsum(jax.nn.softmax(lb_logits.astype(jnp.float32), axis=0), axis=0)
    for l in range(DEPTH):
        h = rmsnorm(x, g_pre[l])
        p = jnp.einsum('bsd,de->bse', h, w_in[l])
        (qa, fa, ia, oga, za, qb, kb, vb, ogb, zb, igb, fgb, gate_a, gate_b) = \
            jnp.split(p, SPLIT_IDX, axis=-1)
        oa = hgrn2_mixer(qa, fa, ia, lb_all[l].astype(x.dtype))
        oa = head_rmsnorm(oa, g_norm_a[l], H_A) * jax.nn.sigmoid(oga) * jax.nn.silu(za)
        qk = jax.nn.silu(causal_conv(jnp.concatenate([qb, kb], axis=-1), conv_w[l], conv_b[l]))
        qb_c, kb_c = qk[..., :D_QK_B], qk[..., D_QK_B:]
        ob = mlstm_mixer(qb_c, kb_c, vb, igb + b_ig[l], fgb + b_fg[l])
        ob = head_rmsnorm(ob, g_norm_b[l], H_B) * jax.nn.sigmoid(ogb) * jax.nn.silu(zb)
        ya = jnp.einsum('bse,ed->bsd', oa, w_up_a[l])
        yb = jnp.einsum('bse,ed->bsd', ob, w_up_b[l])
        y = jax.nn.sigmoid(gate_a) * ya + jax.nn.sigmoid(gate_b) * yb
        out = jnp.einsum('bsd,de->bse', y, w_out[l])
        x = x + rmsnorm(out, g_post[l])
    return x
```

```python
import functools

import jax
import jax.numpy as jnp
from jax import lax
from jax.experimental import pallas as pl
from jax.experimental.pallas import tpu as pltpu

D_MODEL = 4096
D_A = D_MODEL // 2
HA_DK = 128
H_A = D_A // HA_DK
HA_DV = D_A // H_A
D_B = D_MODEL // 2
H_B = 8
DV_B = D_B // H_B
DK_B = DV_B // 2
D_QK_B = H_B * DK_B
CONV_W = 4
CHUNK = 64
EPS = 1e-6
LANES = 128
GATE_PAD = 2 * LANES

F32 = jnp.float32
BF16 = jnp.bfloat16

OFF_QA, OFF_FA, OFF_IA, OFF_OGA, OFF_ZA = (k * D_A for k in range(5))
OFF_QB = 5 * D_A
OFF_KB = OFF_QB + D_QK_B
OFF_VB = OFF_KB + D_QK_B
OFF_OGB = OFF_VB + D_B
OFF_ZB = OFF_OGB + D_B
OFF_GA = OFF_ZB + D_B
OFF_GB = OFF_GA + D_MODEL
P_COLS = OFF_GB + D_MODEL
GATE_COL0 = 5 * D_A + 2 * D_QK_B + 3 * D_B

VMEM_LIMIT = 56 * 1024 * 1024


def _params(sem):
    return pltpu.CompilerParams(dimension_semantics=sem, vmem_limit_bytes=VMEM_LIMIT)


def _sigmoid(x):
    return 1.0 / (1.0 + jnp.exp(-x))


def _log_sigmoid(x):
    return jnp.minimum(x, 0.0) - jnp.log1p(jnp.exp(-jnp.abs(x)))


def _dot(a, b):
    return jnp.dot(a, b, preferred_element_type=F32)


def _dot_nt(a, b):
    return lax.dot_general(a, b, (((1,), (1,)), ((), ())), preferred_element_type=F32)


def _dot_tn(a, b):
    return lax.dot_general(a, b, (((0,), (0,)), ((), ())), preferred_element_type=F32)


def _split_dot(tri, x, left):
    hi = x.astype(BF16)
    lo = (x - hi.astype(F32)).astype(BF16)
    if left:
        return _dot(tri, hi) + _dot(tri, lo)
    return _dot(hi, tri) + _dot(lo, tri)


def _prenorm_kernel(x_ref, g_ref, h_ref):
    x = x_ref[...]
    ms = jnp.mean(x * x, axis=-1, keepdims=True)
    h_ref[...] = ((x * lax.rsqrt(ms + EPS)) * g_ref[...]).astype(BF16)


def _prenorm(x2, g, tm=512):
    t = x2.shape[0]
    return pl.pallas_call(
        _prenorm_kernel,
        grid=(t // tm,),
        in_specs=[pl.BlockSpec((tm, D_MODEL), lambda i: (i, 0)),
                  pl.BlockSpec((1, D_MODEL), lambda i: (0, 0))],
        out_specs=pl.BlockSpec((tm, D_MODEL), lambda i: (i, 0)),
        out_shape=jax.ShapeDtypeStruct((t, D_MODEL), BF16),
        compiler_params=_params(("parallel",)),
        name="prenorm",
    )(x2, g)


def _inproj_kernel(h_ref, w_ref, wg_ref, p_ref, gc_ref, gr_ref):
    h = h_ref[...]
    p_ref[...] = _dot(h, w_ref[...]).astype(BF16)

    @pl.when(pl.program_id(1) == 0)
    def _():
        g = _dot(h, wg_ref[...])
        gc_ref[...] = g
        gr_ref[...] = jnp.concatenate(
            [g[:, :LANES].T[:H_B], g[:, LANES:].T[:H_B]], axis=0)


def _inproj(h, w_main, w_gate, tm=1024, tn=1024):
    t = h.shape[0]
    return pl.pallas_call(
        _inproj_kernel,
        grid=(t // tm, P_COLS // tn),
        in_specs=[pl.BlockSpec((tm, D_MODEL), lambda i, j: (i, 0)),
                  pl.BlockSpec((D_MODEL, tn), lambda i, j: (0, j)),
                  pl.BlockSpec((D_MODEL, GATE_PAD), lambda i, j: (0, 0))],
        out_specs=[pl.BlockSpec((tm, tn), lambda i, j: (i, j)),
                   pl.BlockSpec((tm, GATE_PAD), lambda i, j: (i, 0)),
                   pl.BlockSpec((2 * H_B, tm), lambda i, j: (0, i))],
        out_shape=[jax.ShapeDtypeStruct((t, P_COLS), BF16),
                   jax.ShapeDtypeStruct((t, GATE_PAD), F32),
                   jax.ShapeDtypeStruct((2 * H_B, t), F32)],
        compiler_params=_params(("parallel", "arbitrary")),
        name="inproj",
    )(h, w_main, w_gate)


def _mixer_a_kernel(q_ref, f_ref, i_ref, og_ref, z_ref, lbl_ref, gn_ref, o_ref, st_ref,
                    *, layer, hpb, nchunk):
    @pl.when(pl.program_id(2) == 0)
    def _():
        st_ref[...] = jnp.zeros_like(st_ref)

    lg = lbl_ref[...]
    e = jnp.exp(lg - jnp.max(lg, axis=0, keepdims=True))
    lb = jnp.sum(e[:layer + 1], axis=0, keepdims=True) / jnp.sum(e, axis=0, keepdims=True)
    gn = gn_ref[...]

    row = lax.broadcasted_iota(jnp.int32, (CHUNK, CHUNK), 0)
    col = lax.broadcasted_iota(jnp.int32, (CHUNK, CHUNK), 1)
    causal = row >= col
    tri = jnp.where(causal, 1.0, 0.0).astype(BF16)
    mid = CHUNK // 2 - 1

    def chunk(c, carry):
        rows = pl.ds(pl.multiple_of(c * CHUNK, CHUNK), CHUNK)
        f = lb + (1.0 - lb) * _sigmoid(f_ref[rows, :].astype(F32))
        b = _split_dot(tri, jnp.log(f), left=True)
        k = 1.0 - f
        b_mid = b[mid:mid + 1, :]
        g = b[CHUNK - 1:CHUNK, :]
        q_rel = q_ref[rows, :].astype(F32) * jnp.exp(b - b_mid)
        k_rel = k * jnp.exp(b_mid - b)
        q_in = (q_rel * jnp.exp(b_mid)).astype(BF16)
        k_out = (k_rel * jnp.exp(g - b_mid)).astype(BF16)
        q_rel = q_rel.astype(BF16)
        k_rel = k_rel.astype(BF16)
        eg = jnp.exp(g)
        v = i_ref[rows, :]
        og = og_ref[rows, :].astype(F32)
        z = z_ref[rows, :].astype(F32)
        for h in range(hpb):
            sl = slice(h * HA_DK, (h + 1) * HA_DK)
            s = jnp.where(causal, _dot_nt(q_rel[:, sl], k_rel[:, sl]), 0.0).astype(BF16)
            st = st_ref[h]
            o = _dot(s, v[:, sl]) + _dot_nt(q_in[:, sl], st.astype(BF16))
            st_ref[h] = st * eg[:, sl] + _dot_tn(v[:, sl], k_out[:, sl])
            ms = jnp.mean(o * o, axis=-1, keepdims=True)
            zz = z[:, sl]
            res = (o * lax.rsqrt(ms + EPS)) * gn[:, sl] * _sigmoid(og[:, sl]) * (zz * _sigmoid(zz))
            o_ref[rows, sl] = res.astype(BF16)
        return carry

    lax.fori_loop(0, nchunk, chunk, 0)


def _mixer_a(p, lb_logits, gn_a, layer, batch, seq, tb=512, hpb=4):
    t = p.shape[0]
    cw = hpb * HA_DK
    nb = seq // tb
    ng = D_A // cw

    def spec(off):
        return pl.BlockSpec((tb, cw), lambda b, g, n, o=off // cw: (b * nb + n, o + g))

    return pl.pallas_call(
        functools.partial(_mixer_a_kernel, layer=layer, hpb=hpb, nchunk=tb // CHUNK),
        grid=(batch, ng, nb),
        in_specs=[spec(OFF_QA), spec(OFF_FA), spec(OFF_IA), spec(OFF_OGA), spec(OFF_ZA),
                  pl.BlockSpec((lb_logits.shape[0], cw), lambda b, g, n: (0, g)),
                  pl.BlockSpec((1, cw), lambda b, g, n: (0, g))],
        out_specs=pl.BlockSpec((tb, cw), lambda b, g, n: (b * nb + n, g)),
        out_shape=jax.ShapeDtypeStruct((t, D_A), BF16),
        scratch_shapes=[pltpu.VMEM((hpb, HA_DV, HA_DK), F32)],
        compiler_params=_params(("parallel", "parallel", "arbitrary")),
        name="mixer_a",
    )(p, p, p, p, p, lb_logits, gn_a)


def _mixer_b_kernel(q_ref, k_ref, v_ref, og_ref, z_ref, gc_ref, gr_ref, cw_ref, cb_ref,
                    bgc_ref, bgr_ref, gn_ref, o_ref,
                    conv_ref, c_ref, mrow_ref, mcol_ref, *, nchunk):
    n = pl.program_id(1)

    @pl.when(n == 0)
    def _():
        conv_ref[0:8, :] = jnp.zeros((8, 2 * D_QK_B), F32)
        c_ref[...] = jnp.zeros_like(c_ref)
        mrow_ref[...] = jnp.zeros_like(mrow_ref)
        mcol_ref[...] = jnp.zeros_like(mcol_ref)

    row = lax.broadcasted_iota(jnp.int32, (CHUNK, CHUNK), 0)
    col = lax.broadcasted_iota(jnp.int32, (CHUNK, CHUNK), 1)
    causal = row >= col
    tri_l = jnp.where(causal, 1.0, 0.0).astype(BF16)
    tri_u = jnp.where(row <= col, 1.0, 0.0).astype(BF16)
    rid = lax.broadcasted_iota(jnp.int32, (CHUNK, LANES), 0)
    ones_v = jnp.ones((CHUNK, LANES), BF16)
    cw = cw_ref[...]
    cb = cb_ref[...]
    gn = gn_ref[...]
    bgc = bgc_ref[...]
    bgr = bgr_ref[...]
    kscale = DK_B ** -0.5

    for c in range(nchunk):
        r0 = c * CHUNK
        conv_ref[8:8 + CHUNK, 0:D_QK_B] = q_ref[r0:r0 + CHUNK, :].astype(F32)
        conv_ref[8:8 + CHUNK, D_QK_B:] = k_ref[r0:r0 + CHUNK, :].astype(F32)
        acc = cb
        for s in range(CONV_W):
            acc = acc + cw[CONV_W - 1 - s:CONV_W - s, :] * conv_ref[8 - s:8 - s + CHUNK, :]
        conv_ref[0:8, :] = conv_ref[CHUNK:CHUNK + 8, :]
        qk = acc * _sigmoid(acc)
        qc = qk[:, :D_QK_B]
        kc = qk[:, D_QK_B:] * kscale

        gcol = gc_ref[r0:r0 + CHUNK, :] + bgc
        ig_c = gcol[:, :LANES]
        b_c = _split_dot(tri_l, _log_sigmoid(gcol[:, LANES:]), left=True)
        a_c = ig_c - b_c
        cm = a_c
        for sh in (1, 2, 4, 8, 16, 32):
            cm = jnp.maximum(cm, jnp.where(rid >= sh, pltpu.roll(cm, sh, axis=0), -jnp.inf))
        m_prev_r = mrow_ref[...]
        big_m = jnp.maximum(cm, m_prev_r)
        inter_c = jnp.exp(m_prev_r - big_m)
        enm_c = jnp.exp(-(b_c + big_m))
        m_end_r = big_m[CHUNK - 1:CHUNK, :]
        g_r = b_c[CHUNK - 1:CHUNK, :]
        w_c = jnp.exp(a_c - m_end_r)
        decay_r = jnp.exp(m_prev_r - m_end_r)
        mrow_ref[...] = g_r + m_end_r

        grow = gr_ref[:, r0:r0 + CHUNK] + bgr
        b_r = _split_dot(tri_u, _log_sigmoid(grow[H_B:, :]), left=False)
        a_r = grow[:H_B, :] - b_r
        m_prev_c = mcol_ref[...]
        m_end_c = jnp.maximum(jnp.max(a_r, axis=-1, keepdims=True), m_prev_c)
        mcol_ref[...] = b_r[:, CHUNK - 1:CHUNK] + m_end_c

        for h in range(H_B):
            qh = qc[:, h * DK_B:(h + 1) * DK_B]
            kh = kc[:, h * DK_B:(h + 1) * DK_B]
            vh = v_ref[r0:r0 + CHUNK, h * DV_B:(h + 1) * DV_B]
            v_aug = jnp.concatenate([vh, ones_v], axis=1)
            big_m_h = jnp.broadcast_to(big_m[:, h:h + 1], (CHUNK, CHUNK))
            d = jnp.where(causal, jnp.exp(a_r[h:h + 1, :] - big_m_h), 0.0)
            dw = (d * _dot_nt(qh.astype(BF16), kh.astype(BF16))).astype(BF16)
            qi = (qh * jnp.broadcast_to(inter_c[:, h:h + 1], (CHUNK, DK_B))).astype(BF16)
            c_aug = c_ref[h]
            acc2 = _dot(dw, v_aug) + _dot(qi, c_aug.astype(BF16))
            den = jnp.maximum(jnp.abs(acc2[:, DV_B:]),
                              jnp.broadcast_to(enm_c[:, h:h + 1], (CHUNK, LANES)))
            rden = 1.0 / den
            hh = acc2[:, :DV_B] * jnp.concatenate([rden, rden], axis=1)
            kw = (kh * jnp.broadcast_to(w_c[:, h:h + 1], (CHUNK, DK_B))).astype(BF16)
            dec = jnp.broadcast_to(decay_r[:, h:h + 1], (DK_B, DV_B + LANES))
            c_ref[h] = dec * c_aug + _dot_tn(kw, v_aug)
            ms = jnp.mean(hh * hh, axis=-1, keepdims=True)
            sl = slice(h * DV_B, (h + 1) * DV_B)
            og = og_ref[r0:r0 + CHUNK, sl].astype(F32)
            z = z_ref[r0:r0 + CHUNK, sl].astype(F32)
            res = (hh * lax.rsqrt(ms + EPS)) * gn[:, sl] * _sigmoid(og) * (z * _sigmoid(z))
            o_ref[r0:r0 + CHUNK, sl] = res.astype(BF16)


def _mixer_b(p, gcol, grow, conv_w, conv_b, bg_col, bg_row, gn_b, batch, seq, tb=128):
    t = p.shape[0]
    nb = seq // tb

    def spec(width, off):
        return pl.BlockSpec((tb, width), lambda b, n, o=off // width: (b * nb + n, o))

    def full(shape):
        return pl.BlockSpec(shape, lambda b, n: (0,) * len(shape))

    return pl.pallas_call(
        functools.partial(_mixer_b_kernel, nchunk=tb // CHUNK),
        grid=(batch, nb),
        in_specs=[spec(D_QK_B, OFF_QB), spec(D_QK_B, OFF_KB), spec(D_B, OFF_VB),
                  spec(D_B, OFF_OGB), spec(D_B, OFF_ZB),
                  pl.BlockSpec((tb, GATE_PAD), lambda b, n: (b * nb + n, 0)),
                  pl.BlockSpec((2 * H_B, tb), lambda b, n: (0, b * nb + n)),
                  full((CONV_W, 2 * D_QK_B)), full((1, 2 * D_QK_B)),
                  full((1, GATE_PAD)), full((2 * H_B, 1)), full((1, D_B))],
        out_specs=pl.BlockSpec((tb, D_B), lambda b, n: (b * nb + n, 0)),
        out_shape=jax.ShapeDtypeStruct((t, D_B), BF16),
        scratch_shapes=[pltpu.VMEM((CHUNK + 8, 2 * D_QK_B), F32),
                        pltpu.VMEM((H_B, DK_B, DV_B + LANES), F32),
                        pltpu.VMEM((1, LANES), F32),
                        pltpu.VMEM((H_B, 1), F32)],
        compiler_params=_params(("parallel", "arbitrary")),
        name="mixer_b",
    )(p, p, p, p, p, gcol, grow, conv_w, conv_b, bg_col, bg_row, gn_b)


def _merge_kernel(oa_ref, ob_ref, wa_ref, wb_ref, ga_ref, gb_ref, y_ref):
    ya = _dot(oa_ref[...], wa_ref[...])
    yb = _dot(ob_ref[...], wb_ref[...])
    y = _sigmoid(ga_ref[...].astype(F32)) * ya + _sigmoid(gb_ref[...].astype(F32)) * yb
    y_ref[...] = y.astype(BF16)


def _merge(oa, ob, wa, wb, p, tm=1024, tn=1024):
    t = oa.shape[0]
    return pl.pallas_call(
        _merge_kernel,
        grid=(t // tm, D_MODEL // tn),
        in_specs=[pl.BlockSpec((tm, D_A), lambda i, j: (i, 0)),
                  pl.BlockSpec((tm, D_B), lambda i, j: (i, 0)),
                  pl.BlockSpec((D_A, tn), lambda i, j: (0, j)),
                  pl.BlockSpec((D_B, tn), lambda i, j: (0, j)),
                  pl.BlockSpec((tm, tn), lambda i, j: (i, OFF_GA // tn + j)),
                  pl.BlockSpec((tm, tn), lambda i, j: (i, OFF_GB // tn + j))],
        out_specs=pl.BlockSpec((tm, tn), lambda i, j: (i, j)),
        out_shape=jax.ShapeDtypeStruct((t, D_MODEL), BF16),
        compiler_params=_params(("parallel", "arbitrary")),
        name="merge",
    )(oa, ob, wa, wb, p, p)


def _outproj_kernel(y_ref, w_ref, x_ref, g_ref, o_ref, *, tn):
    j = pl.program_id(1)
    o_ref[:, pl.ds(pl.multiple_of(j * tn, tn), tn)] = _dot(y_ref[...], w_ref[...])

    @pl.when(j == pl.num_programs(1) - 1)
    def _():
        o = o_ref[...]
        ms = jnp.mean(o * o, axis=-1, keepdims=True)
        o_ref[...] = x_ref[...] + (o * lax.rsqrt(ms + EPS)) * g_ref[...]


def _outproj(y, w, x2, g, tm=512, tn=512):
    t = y.shape[0]
    return pl.pallas_call(
        functools.partial(_outproj_kernel, tn=tn),
        grid=(t // tm, D_MODEL // tn),
        in_specs=[pl.BlockSpec((tm, D_MODEL), lambda i, j: (i, 0)),
                  pl.BlockSpec((D_MODEL, tn), lambda i, j: (0, j)),
                  pl.BlockSpec((tm, D_MODEL), lambda i, j: (i, 0)),
                  pl.BlockSpec((1, D_MODEL), lambda i, j: (0, 0))],
        out_specs=pl.BlockSpec((tm, D_MODEL), lambda i, j: (i, 0)),
        out_shape=jax.ShapeDtypeStruct((t, D_MODEL), F32),
        compiler_params=_params(("parallel", "arbitrary")),
        name="outproj",
    )(y, w, x2, g)


def _layer(x2, batch, seq, layer, g_pre, w_in, lb_logits, conv_w, conv_b, b_ig, b_fg,
           g_norm_a, g_norm_b, w_up_a, w_up_b, w_out, g_post):
    w_main = jnp.concatenate([w_in[:, :GATE_COL0], w_in[:, GATE_COL0 + 2 * H_B:]],
                             axis=1).astype(BF16)
    w_gate = jnp.zeros((D_MODEL, GATE_PAD), F32)
    w_gate = w_gate.at[:, :H_B].set(w_in[:, GATE_COL0:GATE_COL0 + H_B])
    w_gate = w_gate.at[:, LANES:LANES + H_B].set(w_in[:, GATE_COL0 + H_B:GATE_COL0 + 2 * H_B])
    w_gate = w_gate.astype(BF16)
    bg_col = jnp.zeros((1, GATE_PAD), F32)
    bg_col = bg_col.at[0, :H_B].set(b_ig).at[0, LANES:LANES + H_B].set(b_fg)
    bg_row = jnp.concatenate([b_ig, b_fg]).reshape(2 * H_B, 1)

    h = _prenorm(x2, g_pre.reshape(1, D_MODEL))
    p, gcol, grow = _inproj(h, w_main, w_gate)
    oa = _mixer_a(p, lb_logits, g_norm_a.reshape(1, D_A), layer, batch, seq)
    ob = _mixer_b(p, gcol, grow, conv_w, conv_b.reshape(1, 2 * D_QK_B), bg_col, bg_row,
                  g_norm_b.reshape(1, D_B), batch, seq)
    y = _merge(oa, ob, w_up_a.astype(BF16), w_up_b.astype(BF16), p)
    return _outproj(y, w_out.astype(BF16), x2, g_post.reshape(1, D_MODEL))


def kernel(x, g_pre, w_in, lb_logits, conv_w, conv_b, b_ig, b_fg, g_norm_a, g_norm_b,
           w_up_a, w_up_b, w_out, g_post):
    batch, seq, _ = x.shape
    depth = g_pre.shape[0]
    x2 = x.reshape(batch * seq, D_MODEL)
    for l in range(depth):
        x2 = _layer(x2, batch, seq, l, g_pre[l], w_in[l], lb_logits, conv_w[l], conv_b[l],
                    b_ig[l], b_fg[l], g_norm_a[l], g_norm_b[l], w_up_a[l], w_up_b[l],
                    w_out[l], g_post[l])
    return x2.reshape(batch, seq, D_MODEL)
```

```python
import functools

import jax
import jax.numpy as jnp
from jax import lax
from jax.experimental import pallas as pl
from jax.experimental.pallas import tpu as pltpu

D_MODEL = 4096
D_A = D_MODEL // 2
HA_DK = 128
H_A = D_A // HA_DK
HA_DV = D_A // H_A
D_B = D_MODEL // 2
H_B = 8
DV_B = D_B // H_B
DK_B = DV_B // 2
D_QK_B = H_B * DK_B
CONV_W = 4
CHUNK = 64
PAIR = 2 * CHUNK
EPS = 1e-6
LANES = 128

F32 = jnp.float32
BF16 = jnp.bfloat16

OFF_QA, OFF_FA, OFF_IA, OFF_OGA, OFF_ZA = (k * D_A for k in range(5))
OFF_QB = 5 * D_A
OFF_KB = OFF_QB + D_QK_B
OFF_VB = OFF_KB + D_QK_B
OFF_OGB = OFF_VB + D_B
OFF_ZB = OFF_OGB + D_B
OFF_GA = OFF_ZB + D_B
OFF_GB = OFF_GA + D_MODEL
P_COLS = OFF_GB + D_MODEL
GATE_COL0 = OFF_GA
N_GATES = 2 * H_B

VMEM_LIMIT = 56 * 1024 * 1024


def _params(sem, flags=None):
    return pltpu.CompilerParams(dimension_semantics=sem, vmem_limit_bytes=VMEM_LIMIT,
                                flags=flags)


MIXER_FLAGS = None


def _sigmoid(x):
    return 1.0 / (1.0 + jnp.exp(-x))


def _log_sigmoid(x):
    return jnp.minimum(x, 0.0) - jnp.log1p(jnp.exp(-jnp.abs(x)))


def _silu_times_sigmoid(z, og):
    return z / ((1.0 + jnp.exp(-z)) * (1.0 + jnp.exp(-og)))


def _dot(a, b):
    return jnp.dot(a, b, preferred_element_type=F32)


def _dot_nt(a, b):
    return lax.dot_general(a, b, (((1,), (1,)), ((), ())), preferred_element_type=F32)


def _dot_tn(a, b):
    return lax.dot_general(a, b, (((0,), (0,)), ((), ())), preferred_element_type=F32)


def _split_dot(tri, x, left):
    hi = x.astype(BF16)
    lo = (x - hi.astype(F32)).astype(BF16)
    if left:
        return _dot(tri, hi) + _dot(tri, lo)
    return _dot(hi, tri) + _dot(lo, tri)


def _pack_kernel(a_ref, b_ref, o_ref, *, n_head_blocks):
    j = pl.program_id(1)

    @pl.when(j < n_head_blocks)
    def _():
        o_ref[...] = a_ref[...].astype(BF16)

    @pl.when(j >= n_head_blocks)
    def _():
        tn = a_ref.shape[1]
        wide = jnp.concatenate([a_ref[...], b_ref[...]], axis=1)
        o_ref[...] = wide[:, N_GATES:N_GATES + tn].astype(BF16)


def _pack_w_in(w_in, tr=512, tn=1024):
    k = w_in.shape[0]
    nhb = GATE_COL0 // tn
    sub = tn // LANES

    def b_map(r, j):
        return (r, jnp.where(j < nhb, (nhb + 1) * sub, (j + 1) * sub))

    return pl.pallas_call(
        functools.partial(_pack_kernel, n_head_blocks=nhb),
        grid=(k // tr, P_COLS // tn),
        in_specs=[pl.BlockSpec((tr, tn), lambda r, j: (r, j)),
                  pl.BlockSpec((tr, LANES), b_map)],
        out_specs=pl.BlockSpec((tr, tn), lambda r, j: (r, j)),
        out_shape=jax.ShapeDtypeStruct((k, P_COLS), BF16),
        compiler_params=_params(("parallel", "arbitrary")),
        name="pack_w_in",
    )(w_in, w_in)


def _prenorm_kernel(x_ref, g_ref, h_ref):
    x = x_ref[...]
    ms = jnp.mean(x * x, axis=-1, keepdims=True)
    h_ref[...] = ((x * lax.rsqrt(ms + EPS)) * g_ref[...]).astype(BF16)


def _prenorm(x2, g, tm=512):
    t = x2.shape[0]
    return pl.pallas_call(
        _prenorm_kernel,
        grid=(t // tm,),
        in_specs=[pl.BlockSpec((tm, D_MODEL), lambda i: (i, 0)),
                  pl.BlockSpec((1, D_MODEL), lambda i: (0, 0))],
        out_specs=pl.BlockSpec((tm, D_MODEL), lambda i: (i, 0)),
        out_shape=jax.ShapeDtypeStruct((t, D_MODEL), BF16),
        compiler_params=_params(("parallel",)),
        name="prenorm",
    )(x2, g)


def _inproj_kernel(h_ref, w_ref, wg_ref, p_ref, gr_ref):
    h = h_ref[...]
    p_ref[...] = _dot(h, w_ref[...]).astype(BF16)

    @pl.when(pl.program_id(1) == 0)
    def _():
        g = _dot(h, wg_ref[...])
        gr_ref[...] = g.T[:N_GATES]


def _inproj(h, w_main, w_gate, tm=1024, tn=1024):
    t = h.shape[0]
    return pl.pallas_call(
        _inproj_kernel,
        grid=(t // tm, P_COLS // tn),
        in_specs=[pl.BlockSpec((tm, D_MODEL), lambda i, j: (i, 0)),
                  pl.BlockSpec((D_MODEL, tn), lambda i, j: (0, j)),
                  pl.BlockSpec((D_MODEL, LANES), lambda i, j: (0, 0))],
        out_specs=[pl.BlockSpec((tm, tn), lambda i, j: (i, j)),
                   pl.BlockSpec((N_GATES, tm), lambda i, j: (0, i))],
        out_shape=[jax.ShapeDtypeStruct((t, P_COLS), BF16),
                   jax.ShapeDtypeStruct((N_GATES, t), F32)],
        compiler_params=_params(("parallel", "arbitrary")),
        name="inproj",
    )(h, w_main, w_gate)


def _mixer_a_kernel(q_ref, f_ref, i_ref, og_ref, z_ref, lbl_ref, gn_ref, o_ref, st_ref,
                    *, layer, hpb, ngroup, gsz):
    @pl.when(pl.program_id(2) == 0)
    def _():
        st_ref[...] = jnp.zeros_like(st_ref)

    lg = lbl_ref[...]
    e = jnp.exp(lg - jnp.max(lg, axis=0, keepdims=True))
    lb = jnp.sum(e[:layer + 1], axis=0, keepdims=True) / jnp.sum(e, axis=0, keepdims=True)
    gn = gn_ref[...]

    row = lax.broadcasted_iota(jnp.int32, (CHUNK, CHUNK), 0)
    col = lax.broadcasted_iota(jnp.int32, (CHUNK, CHUNK), 1)
    causal = row >= col
    tri = jnp.where(causal, 1.0, 0.0).astype(BF16)
    mid = CHUNK // 2 - 1
    heads = [slice(h * HA_DK, (h + 1) * HA_DK) for h in range(hpb)]

    def group(u, carry):
        rows = [pl.ds(pl.multiple_of((u * gsz + c) * CHUNK, CHUNK), CHUNK) for c in range(gsz)]
        fs = [lb + (1.0 - lb) * _sigmoid(f_ref[r, :].astype(F32)) for r in rows]
        bs = [_split_dot(tri, jnp.log(f), left=True) for f in fs]
        q_rel, k_rel, q_in, k_out, eg = [], [], [], [], []
        for c in range(gsz):
            b = bs[c]
            b_mid = b[mid:mid + 1, :]
            g = b[CHUNK - 1:CHUNK, :]
            qr = q_ref[rows[c], :].astype(F32) * jnp.exp(b - b_mid)
            kr = (1.0 - fs[c]) * jnp.exp(b_mid - b)
            q_in.append((qr * jnp.exp(b_mid)).astype(BF16))
            k_out.append((kr * jnp.exp(g - b_mid)).astype(BF16))
            q_rel.append(qr.astype(BF16))
            k_rel.append(kr.astype(BF16))
            eg.append(jnp.exp(g))
        vs = [i_ref[r, :] for r in rows]
        scores = [[jnp.where(causal, _dot_nt(q_rel[c][:, sl], k_rel[c][:, sl]), 0.0).astype(BF16)
                   for sl in heads] for c in range(gsz)]
        upd = [[_dot_tn(vs[c][:, sl], k_out[c][:, sl]) for sl in heads] for c in range(gsz)]
        st = [st_ref[h] for h in range(hpb)]
        for c in range(gsz):
            gate = _silu_times_sigmoid(z_ref[rows[c], :].astype(F32),
                                       og_ref[rows[c], :].astype(F32))
            for h, sl in enumerate(heads):
                o = _dot(scores[c][h], vs[c][:, sl]) + _dot_nt(q_in[c][:, sl], st[h].astype(BF16))
                st[h] = st[h] * eg[c][:, sl] + upd[c][h]
                ms = jnp.mean(o * o, axis=-1, keepdims=True)
                res = (o * lax.rsqrt(ms + EPS)) * gn[:, sl] * gate[:, sl]
                o_ref[rows[c], sl] = res.astype(BF16)
        for h in range(hpb):
            st_ref[h] = st[h]
        return carry

    lax.fori_loop(0, ngroup, group, 0)


def _mixer_a(p, lb_logits, gn_a, layer, batch, seq, tb=512, hpb=4, gsz=4):
    t = p.shape[0]
    cw = hpb * HA_DK
    nb = seq // tb
    ng = D_A // cw

    def spec(off):
        return pl.BlockSpec((tb, cw), lambda b, g, n, o=off // cw: (b * nb + n, o + g))

    return pl.pallas_call(
        functools.partial(_mixer_a_kernel, layer=layer, hpb=hpb,
                          ngroup=tb // (CHUNK * gsz), gsz=gsz),
        grid=(batch, ng, nb),
        in_specs=[spec(OFF_QA), spec(OFF_FA), spec(OFF_IA), spec(OFF_OGA), spec(OFF_ZA),
                  pl.BlockSpec((lb_logits.shape[0], cw), lambda b, g, n: (0, g)),
                  pl.BlockSpec((1, cw), lambda b, g, n: (0, g))],
        out_specs=pl.BlockSpec((tb, cw), lambda b, g, n: (b * nb + n, g)),
        out_shape=jax.ShapeDtypeStruct((t, D_A), BF16),
        scratch_shapes=[pltpu.VMEM((hpb, HA_DV, HA_DK), F32)],
        compiler_params=_params(("parallel", "parallel", "arbitrary"), MIXER_FLAGS),
        name="mixer_a",
    )(p, p, p, p, p, lb_logits, gn_a)


def _mixer_b_kernel(qk_ref, v_ref, og_ref, z_ref, gr_ref, grn_ref, cw_ref, cb_ref, bg_ref, gn_ref,
                    o_ref, prev_ref, c_ref, m_ref, a_scr, colf_scr, dec_scr, *, npair):

    r_sh = lax.broadcasted_iota(jnp.int32, (CHUNK, PAIR), 0)
    c_sh = lax.broadcasted_iota(jnp.int32, (CHUNK, PAIR), 1)
    shift = jnp.concatenate(
        [jnp.where(c_sh == r_sh + (CHUNK - s), 1.0, 0.0) for s in range(1, CONV_W)],
        axis=0).astype(BF16)
    row = lax.broadcasted_iota(jnp.int32, (CHUNK, CHUNK), 0)
    col = lax.broadcasted_iota(jnp.int32, (CHUNK, CHUNK), 1)
    causal = row >= col
    def lane_of(i):
        return jnp.where(i >= CHUNK, i - CHUNK, i)

    r_p = lax.broadcasted_iota(jnp.int32, (PAIR, PAIR), 0)
    c_p = lax.broadcasted_iota(jnp.int32, (PAIR, PAIR), 1)
    tri2 = jnp.where(r_p <= c_p, jnp.where(c_p - r_p <= lane_of(c_p), 1.0, 0.0),
                     0.0).astype(BF16)
    lane = lax.broadcasted_iota(jnp.int32, (H_B, PAIR), 1)
    second = lane >= CHUNK
    lane_in_chunk = lane_of(lane)
    ones_v = jnp.ones((CHUNK, LANES), BF16)
    cw = cw_ref[...]
    cb = cb_ref[...]
    gn = gn_ref[...]
    bg = bg_ref[...]
    kscale = DK_B ** -0.5

    def gates(g, m_prev):
        g = g + bg
        b = _split_dot(tri2, _log_sigmoid(g[H_B:]), left=False)
        a = g[:H_B] - b
        cm = a
        for sft in (1, 2, 4, 8, 16, 32):
            cm = jnp.maximum(cm, jnp.where(lane_in_chunk >= sft,
                                           pltpu.roll(cm, sft, axis=1), -jnp.inf))
        big_m_a_end = jnp.maximum(cm[:, CHUNK - 1:CHUNK], m_prev)
        m_mid = b[:, CHUNK - 1:CHUNK] + big_m_a_end
        m_start = jnp.where(second, m_mid, m_prev)
        big_m = jnp.maximum(cm, m_start)
        big_m_b_end = big_m[:, PAIR - 1:PAIR]
        big_m_end = jnp.where(second, big_m_b_end, big_m_a_end)
        inter = jnp.exp(m_start - big_m)
        enm = jnp.exp(-(b + big_m))
        w = jnp.exp(a - big_m_end)
        colf = jnp.concatenate([big_m, inter, enm, w], axis=0).T
        return (a, colf, jnp.exp(m_prev - big_m_a_end), jnp.exp(m_mid - big_m_b_end),
                b[:, PAIR - 1:PAIR] + big_m_b_end)

    def stash(gq):
        a_scr[...] = gq[0]
        colf_scr[...] = gq[1]
        dec_scr[0] = gq[2]
        dec_scr[1] = gq[3]
        m_ref[...] = gq[4]

    def pair(u, carry):
        base = pl.multiple_of(u * PAIR, PAIR)
        rows = [pl.ds(pl.multiple_of(u * PAIR + c * CHUNK, CHUNK), CHUNK) for c in range(2)]

        cur = qk_ref[pl.ds(base, PAIR), :]
        windows = [jnp.concatenate([prev_ref[...], cur[:CHUNK]], axis=0), cur]
        prev_ref[...] = cur[CHUNK:]
        qk = []
        for c in range(2):
            qk.append([])
            for j in range(2 * H_B):
                ls = slice(j * DK_B, (j + 1) * DK_B)
                sh = _dot(shift, windows[c][:, ls])
                acc = cb[:, ls] + cw[CONV_W - 1:CONV_W, ls] * windows[c][CHUNK:, ls].astype(F32)
                for s in range(1, CONV_W):
                    acc = acc + cw[CONV_W - 1 - s:CONV_W - s, ls] * sh[(s - 1) * CHUNK:s * CHUNK]
                act = acc / (1.0 + jnp.exp(-acc))
                qk[c].append((act if j < H_B else act * kscale).astype(BF16))

        a = a_scr[...]
        colf = colf_scr[...]
        decay = [dec_scr[0], dec_scr[1]]
        nxt_base = pl.multiple_of(jnp.minimum(u + 1, npair - 1) * PAIR, PAIR)
        g_next = jnp.where(u + 1 < npair, gr_ref[:, pl.ds(nxt_base, PAIR)], grn_ref[:, :PAIR])
        nxt = gates(g_next, m_ref[...])

        def colv(c, kind, h, width):
            x = colf[c * CHUNK:(c + 1) * CHUNK, kind * H_B + h:kind * H_B + h + 1]
            return jnp.broadcast_to(x, (CHUNK, width))

        for c in range(2):
            dws, qis, kws = [], [], []
            for h in range(H_B):
                qh, kh = qk[c][h], qk[c][H_B + h]
                a_row = a[h:h + 1, c * CHUNK:(c + 1) * CHUNK]
                d = jnp.where(causal, jnp.exp(a_row - colv(c, 0, h, CHUNK)), 0.0)
                dws.append((d * _dot_nt(qh, kh)).astype(BF16))
                qis.append((qh.astype(F32) * colv(c, 1, h, DK_B)).astype(BF16))
                kws.append((kh.astype(F32) * colv(c, 3, h, DK_B)).astype(BF16))
            for h in range(H_B):
                sl = slice(h * DV_B, (h + 1) * DV_B)
                vaug = jnp.concatenate([v_ref[rows[c], sl], ones_v], axis=1)
                cst = c_ref[h]
                acc2 = _dot(dws[h], vaug) + _dot(qis[h], cst.astype(BF16))
                dec = jnp.broadcast_to(decay[c][h:h + 1, :], (DK_B, DV_B + LANES))
                c_ref[h] = dec * cst + _dot_tn(kws[h], vaug)
                rden = 1.0 / jnp.maximum(jnp.abs(acc2[:, DV_B:]), colv(c, 2, h, LANES))
                hh = acc2[:, :DV_B] * jnp.concatenate([rden, rden], axis=1)
                ms = jnp.mean(hh * hh, axis=-1, keepdims=True)
                gate = _silu_times_sigmoid(z_ref[rows[c], sl].astype(F32),
                                           og_ref[rows[c], sl].astype(F32))
                res = (hh * lax.rsqrt(ms + EPS)) * gn[:, sl] * gate
                o_ref[rows[c], sl] = res.astype(BF16)
        stash(nxt)
        return carry

    @pl.when(pl.program_id(1) == 0)
    def _():
        prev_ref[...] = jnp.zeros_like(prev_ref)
        c_ref[...] = jnp.zeros_like(c_ref)
        stash(gates(gr_ref[:, :PAIR], jnp.zeros((H_B, 1), F32)))

    lax.fori_loop(0, npair, pair, 0)


def _mixer_b(p, grow, conv_w, conv_b, bg, gn_b, batch, seq, tb=512):
    t = p.shape[0]
    nb = seq // tb

    def spec(width, off):
        return pl.BlockSpec((tb, width), lambda b, n, o=off // width: (b * nb + n, o))

    def full(shape):
        return pl.BlockSpec(shape, lambda b, n: (0,) * len(shape))

    return pl.pallas_call(
        functools.partial(_mixer_b_kernel, npair=tb // PAIR),
        grid=(batch, nb),
        in_specs=[spec(2 * D_QK_B, OFF_QB), spec(D_B, OFF_VB),
                  spec(D_B, OFF_OGB), spec(D_B, OFF_ZB),
                  pl.BlockSpec((N_GATES, tb), lambda b, n: (0, b * nb + n)),
                  pl.BlockSpec((N_GATES, tb),
                               lambda b, n: (0, jnp.minimum(b * nb + n + 1, batch * nb - 1))),
                  full((CONV_W, 2 * D_QK_B)), full((1, 2 * D_QK_B)),
                  full((N_GATES, 1)), full((1, D_B))],
        out_specs=pl.BlockSpec((tb, D_B), lambda b, n: (b * nb + n, 0)),
        out_shape=jax.ShapeDtypeStruct((t, D_B), BF16),
        scratch_shapes=[pltpu.VMEM((CHUNK, 2 * D_QK_B), BF16),
                        pltpu.VMEM((H_B, DK_B, DV_B + LANES), F32),
                        pltpu.VMEM((H_B, 1), F32),
                        pltpu.VMEM((H_B, PAIR), F32),
                        pltpu.VMEM((PAIR, 4 * H_B), F32),
                        pltpu.VMEM((2, H_B, 1), F32)],
        compiler_params=_params(("parallel", "arbitrary"), MIXER_FLAGS),
        name="mixer_b",
    )(p, p, p, p, grow, grow, conv_w, conv_b, bg, gn_b)


def _merge_kernel(oa_ref, ob_ref, wa_ref, wb_ref, ga_ref, gb_ref, y_ref):
    ya = _dot(oa_ref[...], wa_ref[...])
    yb = _dot(ob_ref[...], wb_ref[...])
    y = _sigmoid(ga_ref[...].astype(F32)) * ya + _sigmoid(gb_ref[...].astype(F32)) * yb
    y_ref[...] = y.astype(BF16)


def _merge(oa, ob, wa, wb, p, tm=1024, tn=1024):
    t = oa.shape[0]
    return pl.pallas_call(
        _merge_kernel,
        grid=(t // tm, D_MODEL // tn),
        in_specs=[pl.BlockSpec((tm, D_A), lambda i, j: (i, 0)),
                  pl.BlockSpec((tm, D_B), lambda i, j: (i, 0)),
                  pl.BlockSpec((D_A, tn), lambda i, j: (0, j)),
                  pl.BlockSpec((D_B, tn), lambda i, j: (0, j)),
                  pl.BlockSpec((tm, tn), lambda i, j: (i, OFF_GA // tn + j)),
                  pl.BlockSpec((tm, tn), lambda i, j: (i, OFF_GB // tn + j))],
        out_specs=pl.BlockSpec((tm, tn), lambda i, j: (i, j)),
        out_shape=jax.ShapeDtypeStruct((t, D_MODEL), BF16),
        compiler_params=_params(("parallel", "arbitrary")),
        name="merge",
    )(oa, ob, wa, wb, p, p)


def _outproj_kernel(y_ref, w_ref, x_ref, g_ref, o_ref, *, tn):
    j = pl.program_id(1)
    o_ref[:, pl.ds(pl.multiple_of(j * tn, tn), tn)] = _dot(y_ref[...], w_ref[...])

    @pl.when(j == pl.num_programs(1) - 1)
    def _():
        o = o_ref[...]
        ms = jnp.mean(o * o, axis=-1, keepdims=True)
        o_ref[...] = x_ref[...] + (o * lax.rsqrt(ms + EPS)) * g_ref[...]


def _outproj(y, w, x2, g, tm=512, tn=512):
    t = y.shape[0]
    return pl.pallas_call(
        functools.partial(_outproj_kernel, tn=tn),
        grid=(t // tm, D_MODEL // tn),
        in_specs=[pl.BlockSpec((tm, D_MODEL), lambda i, j: (i, 0)),
                  pl.BlockSpec((D_MODEL, tn), lambda i, j: (0, j)),
                  pl.BlockSpec((tm, D_MODEL), lambda i, j: (i, 0)),
                  pl.BlockSpec((1, D_MODEL), lambda i, j: (0, 0))],
        out_specs=pl.BlockSpec((tm, D_MODEL), lambda i, j: (i, 0)),
        out_shape=jax.ShapeDtypeStruct((t, D_MODEL), F32),
        compiler_params=_params(("parallel", "arbitrary")),
        name="outproj",
    )(y, w, x2, g)


def _layer(x2, batch, seq, layer, g_pre, w_in, lb_logits, conv_w, conv_b, b_ig, b_fg,
           g_norm_a, g_norm_b, w_up_a, w_up_b, w_out, g_post):
    w_main = _pack_w_in(w_in)
    w_gate = jnp.pad(w_in[:, GATE_COL0:GATE_COL0 + N_GATES],
                     ((0, 0), (0, LANES - N_GATES))).astype(BF16)
    bg = jnp.concatenate([b_ig, b_fg]).reshape(N_GATES, 1)

    h = _prenorm(x2, g_pre.reshape(1, D_MODEL))
    p, grow = _inproj(h, w_main, w_gate)
    oa = _mixer_a(p, lb_logits, g_norm_a.reshape(1, D_A), layer, batch, seq)
    ob = _mixer_b(p, grow, conv_w, conv_b.reshape(1, 2 * D_QK_B), bg,
                  g_norm_b.reshape(1, D_B), batch, seq)
    y = _merge(oa, ob, w_up_a.astype(BF16), w_up_b.astype(BF16), p)
    return _outproj(y, w_out.astype(BF16), x2, g_post.reshape(1, D_MODEL))


def kernel(x, g_pre, w_in, lb_logits, conv_w, conv_b, b_ig, b_fg, g_norm_a, g_norm_b,
           w_up_a, w_up_b, w_out, g_post):
    batch, seq, _ = x.shape
    depth = g_pre.shape[0]
    x2 = x.reshape(batch * seq, D_MODEL)
    for l in range(depth):
        x2 = _layer(x2, batch, seq, l, g_pre[l], w_in[l], lb_logits, conv_w[l], conv_b[l],
                    b_ig[l], b_fg[l], g_norm_a[l], g_norm_b[l], w_up_a[l], w_up_b[l],
                    w_out[l], g_post[l])
    return x2.reshape(batch, seq, D_MODEL)
```

```python
import functools

import jax
import jax.numpy as jnp
from jax import lax
from jax.experimental import pallas as pl
from jax.experimental.pallas import tpu as pltpu

D_MODEL = 4096
D_A = D_MODEL // 2
HA_DK = 128
H_A = D_A // HA_DK
HA_DV = D_A // H_A
D_B = D_MODEL // 2
H_B = 8
DV_B = D_B // H_B
DK_B = DV_B // 2
D_QK_B = H_B * DK_B
CONV_W = 4
CHUNK = 64
PAIR = 2 * CHUNK
EPS = 1e-6
LANES = 128
SUBLANES = 8

F32 = jnp.float32
BF16 = jnp.bfloat16

OFF_QA, OFF_FA, OFF_IA, OFF_OGA, OFF_ZA = (k * D_A for k in range(5))
OFF_QB = 5 * D_A
OFF_KB = OFF_QB + D_QK_B
OFF_VB = OFF_KB + D_QK_B
OFF_OGB = OFF_VB + D_B
OFF_ZB = OFF_OGB + D_B
OFF_GA = OFF_ZB + D_B
OFF_GB = OFF_GA + D_MODEL
P_COLS = OFF_GB + D_MODEL
GATE_COL0 = OFF_GA
N_GATES = 2 * H_B

VMEM_LIMIT = 60 * 1024 * 1024


def _params(sem, flags=None):
    return pltpu.CompilerParams(dimension_semantics=sem, vmem_limit_bytes=VMEM_LIMIT,
                                flags=flags)


MIXER_FLAGS = None


def _sigmoid(x):
    return 1.0 / (1.0 + jnp.exp(-x))


def _log_sigmoid(x):
    return jnp.minimum(x, 0.0) - jnp.log1p(jnp.exp(-jnp.abs(x)))


def _silu_times_sigmoid(z, og):
    return z / ((1.0 + jnp.exp(-z)) * (1.0 + jnp.exp(-og)))


def _dot(a, b):
    return jnp.dot(a, b, preferred_element_type=F32)


def _dot_nt(a, b):
    return lax.dot_general(a, b, (((1,), (1,)), ((), ())), preferred_element_type=F32)


def _dot_tn(a, b):
    return lax.dot_general(a, b, (((0,), (0,)), ((), ())), preferred_element_type=F32)


def _split_dot(tri, x, left):
    hi = x.astype(BF16)
    lo = (x - hi.astype(F32)).astype(BF16)
    if left:
        return _dot(tri, hi) + _dot(tri, lo)
    return _dot(hi, tri) + _dot(lo, tri)


def _pack_kernel(w_ref, o_ref):
    o_ref[...] = w_ref[...].astype(BF16)


def _pack_w_in_t(w_in_t, tr=512):
    k = w_in_t.shape[1]
    first_tail_block = GATE_COL0 // tr

    def rows(i):
        skip = jnp.where(i >= first_tail_block, N_GATES // SUBLANES, 0)
        return ((i * (tr // SUBLANES) + skip) * SUBLANES, 0)

    return pl.pallas_call(
        _pack_kernel,
        grid=(P_COLS // tr,),
        in_specs=[pl.BlockSpec((pl.Element(tr), pl.Element(k)), rows)],
        out_specs=pl.BlockSpec((tr, k), lambda i: (i, 0)),
        out_shape=jax.ShapeDtypeStruct((P_COLS, k), BF16),
        compiler_params=_params(("parallel",)),
        name="pack_w_in",
    )(w_in_t)


def _prenorm_kernel(x_ref, g_ref, wg_ref, h_ref, gr_ref):
    x = x_ref[...]
    ms = jnp.mean(x * x, axis=-1, keepdims=True)
    h = ((x * lax.rsqrt(ms + EPS)) * g_ref[...]).astype(BF16)
    h_ref[...] = h
    gr_ref[...] = _dot_nt(wg_ref[...], h)


def _prenorm(x2, g, w_gate, tm=512):
    t = x2.shape[0]
    return pl.pallas_call(
        _prenorm_kernel,
        grid=(t // tm,),
        in_specs=[pl.BlockSpec((tm, D_MODEL), lambda i: (i, 0)),
                  pl.BlockSpec((1, D_MODEL), lambda i: (0, 0)),
                  pl.BlockSpec((N_GATES, D_MODEL), lambda i: (0, 0))],
        out_specs=[pl.BlockSpec((tm, D_MODEL), lambda i: (i, 0)),
                   pl.BlockSpec((N_GATES, tm), lambda i: (0, i))],
        out_shape=[jax.ShapeDtypeStruct((t, D_MODEL), BF16),
                   jax.ShapeDtypeStruct((N_GATES, t), F32)],
        compiler_params=_params(("parallel",)),
        name="prenorm",
    )(x2, g, w_gate)


def _inproj_kernel(h_ref, w_ref, p_ref):
    p_ref[...] = _dot_nt(h_ref[...], w_ref[...]).astype(BF16)


def _inproj(h, w_main, tm=1024, tn=1024):
    t = h.shape[0]
    return pl.pallas_call(
        _inproj_kernel,
        grid=(t // tm, P_COLS // tn),
        in_specs=[pl.BlockSpec((tm, D_MODEL), lambda i, j: (i, 0)),
                  pl.BlockSpec((tn, D_MODEL), lambda i, j: (j, 0))],
        out_specs=pl.BlockSpec((tm, tn), lambda i, j: (i, j)),
        out_shape=jax.ShapeDtypeStruct((t, P_COLS), BF16),
        compiler_params=_params(("parallel", "arbitrary")),
        name="inproj",
    )(h, w_main)


def _mixer_a_kernel(q_ref, f_ref, i_ref, og_ref, z_ref, lbl_ref, gn_ref, o_ref, st_ref,
                    *, layer, hpb, ngroup, gsz):
    @pl.when(pl.program_id(2) == 0)
    def _():
        st_ref[...] = jnp.zeros_like(st_ref)

    lg = lbl_ref[...]
    e = jnp.exp(lg - jnp.max(lg, axis=0, keepdims=True))
    lb = jnp.sum(e[:layer + 1], axis=0, keepdims=True) / jnp.sum(e, axis=0, keepdims=True)
    gn = gn_ref[...]

    row = lax.broadcasted_iota(jnp.int32, (CHUNK, CHUNK), 0)
    col = lax.broadcasted_iota(jnp.int32, (CHUNK, CHUNK), 1)
    causal = row >= col
    tri = jnp.where(causal, 1.0, 0.0).astype(BF16)
    mid = CHUNK // 2 - 1
    heads = [slice(h * HA_DK, (h + 1) * HA_DK) for h in range(hpb)]

    def group(u, carry):
        rows = [pl.ds(pl.multiple_of((u * gsz + c) * CHUNK, CHUNK), CHUNK) for c in range(gsz)]
        fs = [lb + (1.0 - lb) * _sigmoid(f_ref[r, :].astype(F32)) for r in rows]
        bs = [_split_dot(tri, jnp.log(f), left=True) for f in fs]
        q_rel, k_rel, q_in, k_out, eg = [], [], [], [], []
        for c in range(gsz):
            b = bs[c]
            b_mid = b[mid:mid + 1, :]
            g = b[CHUNK - 1:CHUNK, :]
            qr = q_ref[rows[c], :].astype(F32) * jnp.exp(b - b_mid)
            kr = (1.0 - fs[c]) * jnp.exp(b_mid - b)
            q_in.append((qr * jnp.exp(b_mid)).astype(BF16))
            k_out.append((kr * jnp.exp(g - b_mid)).astype(BF16))
            q_rel.append(qr.astype(BF16))
            k_rel.append(kr.astype(BF16))
            eg.append(jnp.exp(g))
        vs = [i_ref[r, :] for r in rows]
        scores = [[jnp.where(causal, _dot_nt(q_rel[c][:, sl], k_rel[c][:, sl]), 0.0).astype(BF16)
                   for sl in heads] for c in range(gsz)]
        upd = [[_dot_tn(vs[c][:, sl], k_out[c][:, sl]) for sl in heads] for c in range(gsz)]
        st = [st_ref[h] for h in range(hpb)]
        for c in range(gsz):
            gate = _silu_times_sigmoid(z_ref[rows[c], :].astype(F32),
                                       og_ref[rows[c], :].astype(F32))
            for h, sl in enumerate(heads):
                o = _dot(scores[c][h], vs[c][:, sl]) + _dot_nt(q_in[c][:, sl], st[h].astype(BF16))
                st[h] = st[h] * eg[c][:, sl] + upd[c][h]
                ms = jnp.mean(o * o, axis=-1, keepdims=True)
                res = (o * lax.rsqrt(ms + EPS)) * gn[:, sl] * gate[:, sl]
                o_ref[rows[c], sl] = res.astype(BF16)
        for h in range(hpb):
            st_ref[h] = st[h]
        return carry

    lax.fori_loop(0, ngroup, group, 0)


def _mixer_a(p, lb_logits, gn_a, layer, batch, seq, tb=512, hpb=4, gsz=4):
    t = p.shape[0]
    cw = hpb * HA_DK
    nb = seq // tb
    ng = D_A // cw

    def spec(off):
        return pl.BlockSpec((tb, cw), lambda b, g, n, o=off // cw: (b * nb + n, o + g))

    return pl.pallas_call(
        functools.partial(_mixer_a_kernel, layer=layer, hpb=hpb,
                          ngroup=tb // (CHUNK * gsz), gsz=gsz),
        grid=(batch, ng, nb),
        in_specs=[spec(OFF_QA), spec(OFF_FA), spec(OFF_IA), spec(OFF_OGA), spec(OFF_ZA),
                  pl.BlockSpec((lb_logits.shape[0], cw), lambda b, g, n: (0, g)),
                  pl.BlockSpec((1, cw), lambda b, g, n: (0, g))],
        out_specs=pl.BlockSpec((tb, cw), lambda b, g, n: (b * nb + n, g)),
        out_shape=jax.ShapeDtypeStruct((t, D_A), BF16),
        scratch_shapes=[pltpu.VMEM((hpb, HA_DV, HA_DK), F32)],
        compiler_params=_params(("parallel", "parallel", "arbitrary"), MIXER_FLAGS),
        name="mixer_a",
    )(p, p, p, p, p, lb_logits, gn_a)


def _mixer_b_kernel(qk_ref, v_ref, og_ref, z_ref, gr_ref, grn_ref, cw_ref, cb_ref, bg_ref, gn_ref,
                    o_ref, prev_ref, c_ref, m_ref, a_scr, colf_scr, dec_scr, *, npair):

    r_sh = lax.broadcasted_iota(jnp.int32, (CHUNK, PAIR), 0)
    c_sh = lax.broadcasted_iota(jnp.int32, (CHUNK, PAIR), 1)
    shift = jnp.concatenate(
        [jnp.where(c_sh == r_sh + (CHUNK - s), 1.0, 0.0) for s in range(1, CONV_W)],
        axis=0).astype(BF16)
    row = lax.broadcasted_iota(jnp.int32, (CHUNK, CHUNK), 0)
    col = lax.broadcasted_iota(jnp.int32, (CHUNK, CHUNK), 1)
    causal = row >= col
    def lane_of(i):
        return jnp.where(i >= CHUNK, i - CHUNK, i)

    r_p = lax.broadcasted_iota(jnp.int32, (PAIR, PAIR), 0)
    c_p = lax.broadcasted_iota(jnp.int32, (PAIR, PAIR), 1)
    tri2 = jnp.where(r_p <= c_p, jnp.where(c_p - r_p <= lane_of(c_p), 1.0, 0.0),
                     0.0).astype(BF16)
    lane = lax.broadcasted_iota(jnp.int32, (H_B, PAIR), 1)
    second = lane >= CHUNK
    lane_in_chunk = lane_of(lane)
    ones_v = jnp.ones((CHUNK, LANES), BF16)
    cw = cw_ref[...]
    cb = cb_ref[...]
    gn = gn_ref[...]
    bg = bg_ref[...]
    kscale = DK_B ** -0.5

    def gates(g, m_prev):
        g = g + bg
        b = _split_dot(tri2, _log_sigmoid(g[H_B:]), left=False)
        a = g[:H_B] - b
        cm = a
        for sft in (1, 2, 4, 8, 16, 32):
            cm = jnp.maximum(cm, jnp.where(lane_in_chunk >= sft,
                                           pltpu.roll(cm, sft, axis=1), -jnp.inf))
        big_m_a_end = jnp.maximum(cm[:, CHUNK - 1:CHUNK], m_prev)
        m_mid = b[:, CHUNK - 1:CHUNK] + big_m_a_end
        m_start = jnp.where(second, m_mid, m_prev)
        big_m = jnp.maximum(cm, m_start)
        big_m_b_end = big_m[:, PAIR - 1:PAIR]
        big_m_end = jnp.where(second, big_m_b_end, big_m_a_end)
        inter = jnp.exp(m_start - big_m)
        enm = jnp.exp(-(b + big_m))
        w = jnp.exp(a - big_m_end)
        colf = jnp.concatenate([big_m, inter, enm, w], axis=0).T
        return (a, colf, jnp.exp(m_prev - big_m_a_end), jnp.exp(m_mid - big_m_b_end),
                b[:, PAIR - 1:PAIR] + big_m_b_end)

    def stash(gq):
        a_scr[...] = gq[0]
        colf_scr[...] = gq[1]
        dec_scr[0] = gq[2]
        dec_scr[1] = gq[3]
        m_ref[...] = gq[4]

    def pair(u, carry):
        base = pl.multiple_of(u * PAIR, PAIR)
        rows = [pl.ds(pl.multiple_of(u * PAIR + c * CHUNK, CHUNK), CHUNK) for c in range(2)]

        cur = qk_ref[pl.ds(base, PAIR), :]
        windows = [jnp.concatenate([prev_ref[...], cur[:CHUNK]], axis=0), cur]
        prev_ref[...] = cur[CHUNK:]
        qk = []
        for c in range(2):
            qk.append([])
            for j in range(2 * H_B):
                ls = slice(j * DK_B, (j + 1) * DK_B)
                sh = _dot(shift, windows[c][:, ls])
                acc = cb[:, ls] + cw[CONV_W - 1:CONV_W, ls] * windows[c][CHUNK:, ls].astype(F32)
                for s in range(1, CONV_W):
                    acc = acc + cw[CONV_W - 1 - s:CONV_W - s, ls] * sh[(s - 1) * CHUNK:s * CHUNK]
                act = acc / (1.0 + jnp.exp(-acc))
                qk[c].append((act if j < H_B else act * kscale).astype(BF16))

        a = a_scr[...]
        colf = colf_scr[...]
        decay = [dec_scr[0], dec_scr[1]]
        nxt_base = pl.multiple_of(jnp.minimum(u + 1, npair - 1) * PAIR, PAIR)
        g_next = jnp.where(u + 1 < npair, gr_ref[:, pl.ds(nxt_base, PAIR)], grn_ref[:, :PAIR])
        nxt = gates(g_next, m_ref[...])

        def colv(c, kind, h, width):
            x = colf[c * CHUNK:(c + 1) * CHUNK, kind * H_B + h:kind * H_B + h + 1]
            return jnp.broadcast_to(x, (CHUNK, width))

        for c in range(2):
            dws, qis, kws = [], [], []
            for h in range(H_B):
                qh, kh = qk[c][h], qk[c][H_B + h]
                a_row = a[h:h + 1, c * CHUNK:(c + 1) * CHUNK]
                d = jnp.where(causal, jnp.exp(a_row - colv(c, 0, h, CHUNK)), 0.0)
                dws.append((d * _dot_nt(qh, kh)).astype(BF16))
                qis.append((qh.astype(F32) * colv(c, 1, h, DK_B)).astype(BF16))
                kws.append((kh.astype(F32) * colv(c, 3, h, DK_B)).astype(BF16))
            for h in range(H_B):
                sl = slice(h * DV_B, (h + 1) * DV_B)
                vaug = jnp.concatenate([v_ref[rows[c], sl], ones_v], axis=1)
                cst = c_ref[h]
                acc2 = _dot(dws[h], vaug) + _dot(qis[h], cst.astype(BF16))
                dec = jnp.broadcast_to(decay[c][h:h + 1, :], (DK_B, DV_B + LANES))
                c_ref[h] = dec * cst + _dot_tn(kws[h], vaug)
                rden = 1.0 / jnp.maximum(jnp.abs(acc2[:, DV_B:]), colv(c, 2, h, LANES))
                hh = acc2[:, :DV_B] * jnp.concatenate([rden, rden], axis=1)
                ms = jnp.mean(hh * hh, axis=-1, keepdims=True)
                gate = _silu_times_sigmoid(z_ref[rows[c], sl].astype(F32),
                                           og_ref[rows[c], sl].astype(F32))
                res = (hh * lax.rsqrt(ms + EPS)) * gn[:, sl] * gate
                o_ref[rows[c], sl] = res.astype(BF16)
        stash(nxt)
        return carry

    @pl.when(pl.program_id(1) == 0)
    def _():
        prev_ref[...] = jnp.zeros_like(prev_ref)
        c_ref[...] = jnp.zeros_like(c_ref)
        stash(gates(gr_ref[:, :PAIR], jnp.zeros((H_B, 1), F32)))

    lax.fori_loop(0, npair, pair, 0)


def _mixer_b(p, grow, conv_w, conv_b, bg, gn_b, batch, seq, tb=512):
    t = p.shape[0]
    nb = seq // tb

    def spec(width, off):
        return pl.BlockSpec((tb, width), lambda b, n, o=off // width: (b * nb + n, o))

    def full(shape):
        return pl.BlockSpec(shape, lambda b, n: (0,) * len(shape))

    return pl.pallas_call(
        functools.partial(_mixer_b_kernel, npair=tb // PAIR),
        grid=(batch, nb),
        in_specs=[spec(2 * D_QK_B, OFF_QB), spec(D_B, OFF_VB),
                  spec(D_B, OFF_OGB), spec(D_B, OFF_ZB),
                  pl.BlockSpec((N_GATES, tb), lambda b, n: (0, b * nb + n)),
                  pl.BlockSpec((N_GATES, tb),
                               lambda b, n: (0, jnp.minimum(b * nb + n + 1, batch * nb - 1))),
                  full((CONV_W, 2 * D_QK_B)), full((1, 2 * D_QK_B)),
                  full((N_GATES, 1)), full((1, D_B))],
        out_specs=pl.BlockSpec((tb, D_B), lambda b, n: (b * nb + n, 0)),
        out_shape=jax.ShapeDtypeStruct((t, D_B), BF16),
        scratch_shapes=[pltpu.VMEM((CHUNK, 2 * D_QK_B), BF16),
                        pltpu.VMEM((H_B, DK_B, DV_B + LANES), F32),
                        pltpu.VMEM((H_B, 1), F32),
                        pltpu.VMEM((H_B, PAIR), F32),
                        pltpu.VMEM((PAIR, 4 * H_B), F32),
                        pltpu.VMEM((2, H_B, 1), F32)],
        compiler_params=_params(("parallel", "arbitrary"), MIXER_FLAGS),
        name="mixer_b",
    )(p, p, p, p, grow, grow, conv_w, conv_b, bg, gn_b)


def _merge_kernel(oa_ref, ob_ref, wa_ref, wb_ref, ga_ref, gb_ref, y_ref):
    ya = _dot(oa_ref[...], wa_ref[...])
    yb = _dot(ob_ref[...], wb_ref[...])
    y = _sigmoid(ga_ref[...].astype(F32)) * ya + _sigmoid(gb_ref[...].astype(F32)) * yb
    y_ref[...] = y.astype(BF16)


def _merge(oa, ob, wa, wb, p, tm=1024, tn=1024):
    t = oa.shape[0]
    return pl.pallas_call(
        _merge_kernel,
        grid=(t // tm, D_MODEL // tn),
        in_specs=[pl.BlockSpec((tm, D_A), lambda i, j: (i, 0)),
                  pl.BlockSpec((tm, D_B), lambda i, j: (i, 0)),
                  pl.BlockSpec((D_A, tn), lambda i, j: (0, j)),
                  pl.BlockSpec((D_B, tn), lambda i, j: (0, j)),
                  pl.BlockSpec((tm, tn), lambda i, j: (i, OFF_GA // tn + j)),
                  pl.BlockSpec((tm, tn), lambda i, j: (i, OFF_GB // tn + j))],
        out_specs=pl.BlockSpec((tm, tn), lambda i, j: (i, j)),
        out_shape=jax.ShapeDtypeStruct((t, D_MODEL), BF16),
        compiler_params=_params(("parallel", "arbitrary")),
        name="merge",
    )(oa, ob, wa, wb, p, p)


def _outproj_kernel(y_ref, w_ref, x_ref, g_ref, o_ref, ss_ref, *, tn):
    j = pl.program_id(1)
    o = _dot(y_ref[...], w_ref[...])
    o_ref[:, pl.ds(pl.multiple_of(j * tn, tn), tn)] = o
    sq = o * o
    part = sq[:, :LANES]
    for c in range(1, tn // LANES):
        part = part + sq[:, c * LANES:(c + 1) * LANES]

    ss_ref[...] = jnp.where(j == 0, part, ss_ref[...] + part)

    @pl.when(j == pl.num_programs(1) - 1)
    def _():
        ms = jnp.sum(ss_ref[...], axis=-1, keepdims=True) * (1.0 / D_MODEL)
        o_ref[...] = x_ref[...] + (o_ref[...] * lax.rsqrt(ms + EPS)) * g_ref[...]


def _outproj(y, w, x2, g, tm=512, tn=512):
    t = y.shape[0]
    return pl.pallas_call(
        functools.partial(_outproj_kernel, tn=tn),
        grid=(t // tm, D_MODEL // tn),
        in_specs=[pl.BlockSpec((tm, D_MODEL), lambda i, j: (i, 0)),
                  pl.BlockSpec((D_MODEL, tn), lambda i, j: (0, j)),
                  pl.BlockSpec((tm, D_MODEL), lambda i, j: (i, 0)),
                  pl.BlockSpec((1, D_MODEL), lambda i, j: (0, 0))],
        out_specs=pl.BlockSpec((tm, D_MODEL), lambda i, j: (i, 0)),
        out_shape=jax.ShapeDtypeStruct((t, D_MODEL), F32),
        scratch_shapes=[pltpu.VMEM((tm, LANES), F32)],
        compiler_params=_params(("parallel", "arbitrary")),
        name="outproj",
    )(y, w, x2, g)


def _layer(x2, batch, seq, layer, g_pre, w_in, lb_logits, conv_w, conv_b, b_ig, b_fg,
           g_norm_a, g_norm_b, w_up_a, w_up_b, w_out, g_post):
    w_in_t = w_in.T
    w_main = _pack_w_in_t(w_in_t)
    w_gate = w_in_t[GATE_COL0:GATE_COL0 + N_GATES].astype(BF16)
    bg = jnp.concatenate([b_ig, b_fg]).reshape(N_GATES, 1)

    h, grow = _prenorm(x2, g_pre.reshape(1, D_MODEL), w_gate)
    p = _inproj(h, w_main)
    oa = _mixer_a(p, lb_logits, g_norm_a.reshape(1, D_A), layer, batch, seq)
    ob = _mixer_b(p, grow, conv_w, conv_b.reshape(1, 2 * D_QK_B), bg,
                  g_norm_b.reshape(1, D_B), batch, seq)
    y = _merge(oa, ob, w_up_a.astype(BF16), w_up_b.astype(BF16), p)
    return _outproj(y, w_out.astype(BF16), x2, g_post.reshape(1, D_MODEL))


def kernel(x, g_pre, w_in, lb_logits, conv_w, conv_b, b_ig, b_fg, g_norm_a, g_norm_b,
           w_up_a, w_up_b, w_out, g_post):
    batch, seq, _ = x.shape
    depth = g_pre.shape[0]
    x2 = x.reshape(batch * seq, D_MODEL)
    for l in range(depth):
        x2 = _layer(x2, batch, seq, l, g_pre[l], w_in[l], lb_logits, conv_w[l], conv_b[l],
                    b_ig[l], b_fg[l], g_norm_a[l], g_norm_b[l], w_up_a[l], w_up_b[l],
                    w_out[l], g_post[l])
    return x2.reshape(batch, seq, D_MODEL)
```

```python
import functools
import math

import jax
import jax.numpy as jnp
from jax import lax
from jax.experimental import pallas as pl
from jax.experimental.pallas import tpu as pltpu

D_MODEL = 4096
D_A = D_MODEL // 2
HA_DK = 128
H_A = D_A // HA_DK
HA_DV = D_A // H_A
D_B = D_MODEL // 2
H_B = 8
DV_B = D_B // H_B
DK_B = DV_B // 2
D_QK_B = H_B * DK_B
CONV_W = 4
CHUNK = 64
PAIR = 2 * CHUNK
EPS = 1e-6
LANES = 128
SUBLANES = 8

F32 = jnp.float32
BF16 = jnp.bfloat16

OFF_QA, OFF_FA, OFF_IA, OFF_OGA, OFF_ZA = (k * D_A for k in range(5))
OFF_QB = 5 * D_A
OFF_KB = OFF_QB + D_QK_B
OFF_VB = OFF_KB + D_QK_B
OFF_OGB = OFF_VB + D_B
OFF_ZB = OFF_OGB + D_B
OFF_GA = OFF_ZB + D_B
OFF_GB = OFF_GA + D_MODEL
P_COLS = OFF_GB + D_MODEL
GATE_COL0 = OFF_GA
N_GATES = 2 * H_B

VMEM_LIMIT = 60 * 1024 * 1024


def _params(sem, flags=None):
    return pltpu.CompilerParams(dimension_semantics=sem, vmem_limit_bytes=VMEM_LIMIT,
                                flags=flags)


MIXER_FLAGS = None


def _sigmoid(x):
    return 1.0 / (1.0 + jnp.exp(-x))


def _log_sigmoid(x):
    return jnp.minimum(x, 0.0) - jnp.log1p(jnp.exp(-jnp.abs(x)))


def _silu_times_sigmoid(z, og):
    return z / ((1.0 + jnp.exp(-z)) * (1.0 + jnp.exp(-og)))


def _dot(a, b):
    return jnp.dot(a, b, preferred_element_type=F32)


def _dot_nt(a, b):
    return lax.dot_general(a, b, (((1,), (1,)), ((), ())), preferred_element_type=F32)


def _dot_tn(a, b):
    return lax.dot_general(a, b, (((0,), (0,)), ((), ())), preferred_element_type=F32)


def _split_dot(tri, x, left):
    hi = x.astype(BF16)
    lo = (x - hi.astype(F32)).astype(BF16)
    if left:
        return _dot(tri, hi) + _dot(tri, lo)
    return _dot(hi, tri) + _dot(lo, tri)


def _prenorm_kernel(x_ref, g_ref, wg_ref, h_ref, gr_ref):
    x = x_ref[...]
    ms = jnp.mean(x * x, axis=-1, keepdims=True)
    h = ((x * lax.rsqrt(ms + EPS)) * g_ref[...]).astype(BF16)
    h_ref[...] = h
    gr_ref[...] = _dot_nt(wg_ref[...], h)


def _prenorm(x2, g, w_gate, tm=512):
    t = x2.shape[0]
    return pl.pallas_call(
        _prenorm_kernel,
        grid=(t // tm,),
        in_specs=[pl.BlockSpec((tm, D_MODEL), lambda i: (i, 0)),
                  pl.BlockSpec((1, D_MODEL), lambda i: (0, 0)),
                  pl.BlockSpec((N_GATES, D_MODEL), lambda i: (0, 0))],
        out_specs=[pl.BlockSpec((tm, D_MODEL), lambda i: (i, 0)),
                   pl.BlockSpec((N_GATES, tm), lambda i: (0, i))],
        out_shape=[jax.ShapeDtypeStruct((t, D_MODEL), BF16),
                   jax.ShapeDtypeStruct((N_GATES, t), F32)],
        compiler_params=_params(("parallel",)),
        name="prenorm",
    )(x2, g, w_gate)


def _inproj_kernel(h_ref, wt_hbm, p_ref, wbuf, stage, sem, *, n_i, n_j, tn, first_tail):
    j = pl.program_id(0)
    i = pl.program_id(1)
    step = j * n_i + i
    rc = tn // n_i

    def chunk_copy(tile, c, slot):
        tile = jnp.minimum(tile, n_j - 1)
        skip = jnp.where(tile >= first_tail, N_GATES // SUBLANES, 0)
        row = (tile * (tn // SUBLANES) + c * (rc // SUBLANES) + skip) * SUBLANES
        return pltpu.make_async_copy(wt_hbm.at[pl.ds(row, rc), :], stage.at[slot], sem.at[slot])

    def following(tile, c):
        return jnp.where(c + 1 < n_i, tile, tile + 1), jnp.where(c + 1 < n_i, c + 1, 0)

    @pl.when(step == 0)
    def _():
        for c in range(n_i):
            cp = chunk_copy(0, c, c % 2)
            cp.start()
            cp.wait()
            wbuf[0, c * rc:(c + 1) * rc, :] = stage[c % 2].astype(BF16)
        cp = chunk_copy(1, 0, 0)
        cp.start()
        cp.wait()

    slot = lax.rem(step, 2)
    nxt_tile, nxt_c = following(j + 1, i)
    nxt = chunk_copy(nxt_tile, nxt_c, 1 - slot)
    nxt.start()
    p_ref[...] = _dot_nt(h_ref[...], wbuf[lax.rem(j, 2)]).astype(BF16)
    wbuf[lax.rem(j + 1, 2), pl.ds(pl.multiple_of(i * rc, rc), rc), :] = stage[slot].astype(BF16)
    nxt.wait()


def _inproj(h, w_in_t, tm=1024, tn=1024):
    t, k = h.shape
    n_i, n_j = t // tm, P_COLS // tn
    return pl.pallas_call(
        functools.partial(_inproj_kernel, n_i=n_i, n_j=n_j, tn=tn, first_tail=GATE_COL0 // tn),
        grid=(n_j, n_i),
        in_specs=[pl.BlockSpec((tm, k), lambda j, i: (i, 0)),
                  pl.BlockSpec(memory_space=pl.ANY)],
        out_specs=pl.BlockSpec((tm, tn), lambda j, i: (i, j)),
        out_shape=jax.ShapeDtypeStruct((t, P_COLS), BF16),
        scratch_shapes=[pltpu.VMEM((2, tn, k), BF16),
                        pltpu.VMEM((2, tn // n_i, k), F32),
                        pltpu.SemaphoreType.DMA((2,))],
        compiler_params=_params(("arbitrary", "arbitrary")),
        name="inproj",
    )(h, w_in_t)


def _mixer_a_kernel(q_ref, f_ref, i_ref, og_ref, z_ref, lbl_ref, gn_ref, o_ref, st_ref,
                    *, layer, hpb, ngroup, gsz):
    @pl.when(pl.program_id(2) == 0)
    def _():
        st_ref[...] = jnp.zeros_like(st_ref)

    lg = lbl_ref[...]
    e = jnp.exp(lg - jnp.max(lg, axis=0, keepdims=True))
    lb = jnp.sum(e[:layer + 1], axis=0, keepdims=True) / jnp.sum(e, axis=0, keepdims=True)
    gn = gn_ref[...]

    row = lax.broadcasted_iota(jnp.int32, (CHUNK, CHUNK), 0)
    col = lax.broadcasted_iota(jnp.int32, (CHUNK, CHUNK), 1)
    causal = row >= col
    tri = jnp.where(causal, 1.0, 0.0).astype(BF16)
    mid = CHUNK // 2 - 1
    heads = [slice(h * HA_DK, (h + 1) * HA_DK) for h in range(hpb)]

    def group(u, carry):
        rows = [pl.ds(pl.multiple_of((u * gsz + c) * CHUNK, CHUNK), CHUNK) for c in range(gsz)]
        fs = [lb + (1.0 - lb) * _sigmoid(f_ref[r, :].astype(F32)) for r in rows]
        bs = [_split_dot(tri, jnp.log(f), left=True) for f in fs]
        q_rel, k_rel, q_in, k_out, eg = [], [], [], [], []
        for c in range(gsz):
            b = bs[c]
            b_mid = b[mid:mid + 1, :]
            g = b[CHUNK - 1:CHUNK, :]
            e_rel = jnp.exp(b - b_mid)
            qr = q_ref[rows[c], :].astype(F32) * e_rel
            kr = (1.0 - fs[c]) / e_rel
            q_in.append((qr * jnp.exp(b_mid)).astype(BF16))
            k_out.append((kr * jnp.exp(g - b_mid)).astype(BF16))
            q_rel.append(qr.astype(BF16))
            k_rel.append(kr.astype(BF16))
            eg.append(jnp.exp(g))
        vs = [i_ref[r, :] for r in rows]
        scores = [[jnp.where(causal, _dot_nt(q_rel[c][:, sl], k_rel[c][:, sl]), 0.0).astype(BF16)
                   for sl in heads] for c in range(gsz)]
        upd = [[_dot_tn(vs[c][:, sl], k_out[c][:, sl]) for sl in heads] for c in range(gsz)]
        st = [st_ref[h] for h in range(hpb)]
        for c in range(gsz):
            gate = _silu_times_sigmoid(z_ref[rows[c], :].astype(F32),
                                       og_ref[rows[c], :].astype(F32))
            for h, sl in enumerate(heads):
                o = _dot(scores[c][h], vs[c][:, sl]) + _dot_nt(q_in[c][:, sl], st[h].astype(BF16))
                st[h] = st[h] * eg[c][:, sl] + upd[c][h]
                ms = jnp.mean(o * o, axis=-1, keepdims=True)
                res = (o * lax.rsqrt(ms + EPS)) * gn[:, sl] * gate[:, sl]
                o_ref[rows[c], sl] = res.astype(BF16)
        for h in range(hpb):
            st_ref[h] = st[h]
        return carry

    lax.fori_loop(0, ngroup, group, 0)


def _mixer_a(p, lb_logits, gn_a, layer, batch, seq, tb=512, hpb=4, gsz=4):
    t = p.shape[0]
    cw = hpb * HA_DK
    nb = seq // tb
    ng = D_A // cw

    def spec(off):
        return pl.BlockSpec((tb, cw), lambda b, g, n, o=off // cw: (b * nb + n, o + g))

    return pl.pallas_call(
        functools.partial(_mixer_a_kernel, layer=layer, hpb=hpb,
                          ngroup=tb // (CHUNK * gsz), gsz=gsz),
        grid=(batch, ng, nb),
        in_specs=[spec(OFF_QA), spec(OFF_FA), spec(OFF_IA), spec(OFF_OGA), spec(OFF_ZA),
                  pl.BlockSpec((lb_logits.shape[0], cw), lambda b, g, n: (0, g)),
                  pl.BlockSpec((1, cw), lambda b, g, n: (0, g))],
        out_specs=pl.BlockSpec((tb, cw), lambda b, g, n: (b * nb + n, g)),
        out_shape=jax.ShapeDtypeStruct((t, D_A), BF16),
        scratch_shapes=[pltpu.VMEM((hpb, HA_DV, HA_DK), F32)],
        compiler_params=_params(("parallel", "parallel", "arbitrary"), MIXER_FLAGS),
        name="mixer_a",
    )(p, p, p, p, p, lb_logits, gn_a)


def _mixer_b_kernel(qk_ref, v_ref, og_ref, z_ref, gr_ref, grn_ref, cw_ref, cb_ref, bg_ref, gn_ref,
                    o_ref, prev_ref, c_ref, m_ref, a_scr, colf_scr, dec_scr, *, npair):

    r_sh = lax.broadcasted_iota(jnp.int32, (CHUNK, PAIR), 0)
    c_sh = lax.broadcasted_iota(jnp.int32, (CHUNK, PAIR), 1)
    shift = jnp.concatenate(
        [jnp.where(c_sh == r_sh + (CHUNK - s), 1.0, 0.0) for s in range(1, CONV_W)],
        axis=0).astype(BF16)
    row = lax.broadcasted_iota(jnp.int32, (CHUNK, CHUNK), 0)
    col = lax.broadcasted_iota(jnp.int32, (CHUNK, CHUNK), 1)
    causal = row >= col
    def lane_of(i):
        return jnp.where(i >= CHUNK, i - CHUNK, i)

    r_p = lax.broadcasted_iota(jnp.int32, (PAIR, PAIR), 0)
    c_p = lax.broadcasted_iota(jnp.int32, (PAIR, PAIR), 1)
    tri2 = jnp.where(r_p <= c_p, jnp.where(c_p - r_p <= lane_of(c_p), 1.0, 0.0),
                     0.0).astype(BF16)
    lane = lax.broadcasted_iota(jnp.int32, (H_B, PAIR), 1)
    second = lane >= CHUNK
    lane_in_chunk = lane_of(lane)
    ones_v = jnp.ones((CHUNK, LANES), BF16)
    cw = cw_ref[...]
    cb = cb_ref[...]
    gn = gn_ref[...]
    bg = bg_ref[...]
    log_kscale = -0.5 * math.log(DK_B)

    def gates(g, m_prev):
        g = g + bg
        b = _split_dot(tri2, _log_sigmoid(g[H_B:]), left=False)
        a = g[:H_B] - b
        cm = a
        for sft in (1, 2, 4, 8, 16, 32):
            cm = jnp.maximum(cm, jnp.where(lane_in_chunk >= sft,
                                           pltpu.roll(cm, sft, axis=1), -jnp.inf))
        big_m_a_end = jnp.maximum(cm[:, CHUNK - 1:CHUNK], m_prev)
        m_mid = b[:, CHUNK - 1:CHUNK] + big_m_a_end
        m_start = jnp.where(second, m_mid, m_prev)
        big_m = jnp.maximum(cm, m_start)
        big_m_b_end = big_m[:, PAIR - 1:PAIR]
        big_m_end = jnp.where(second, big_m_b_end, big_m_a_end)
        inter = jnp.exp(m_start - big_m)
        enm = jnp.exp(-(b + big_m))
        a_k = a + log_kscale
        w = jnp.exp(a_k - big_m_end)
        colf = jnp.concatenate([big_m, inter, enm, w], axis=0).T
        return (a_k, colf, jnp.exp(m_prev - big_m_a_end), jnp.exp(m_mid - big_m_b_end),
                b[:, PAIR - 1:PAIR] + big_m_b_end)

    def stash(gq):
        a_scr[...] = gq[0]
        colf_scr[...] = gq[1]
        dec_scr[0] = gq[2]
        dec_scr[1] = gq[3]
        m_ref[...] = gq[4]

    def pair(u, carry):
        base = pl.multiple_of(u * PAIR, PAIR)
        rows = [pl.ds(pl.multiple_of(u * PAIR + c * CHUNK, CHUNK), CHUNK) for c in range(2)]

        cur = qk_ref[pl.ds(base, PAIR), :]
        windows = [jnp.concatenate([prev_ref[...], cur[:CHUNK]], axis=0), cur]
        prev_ref[...] = cur[CHUNK:]
        qk = []
        for c in range(2):
            qk.append([])
            for j in range(2 * H_B):
                ls = slice(j * DK_B, (j + 1) * DK_B)
                sh = _dot(shift, windows[c][:, ls])
                acc = cb[:, ls] + cw[CONV_W - 1:CONV_W, ls] * windows[c][CHUNK:, ls].astype(F32)
                for s in range(1, CONV_W):
                    acc = acc + cw[CONV_W - 1 - s:CONV_W - s, ls] * sh[(s - 1) * CHUNK:s * CHUNK]
                act = acc / (1.0 + jnp.exp(-acc))
                qk[c].append(act.astype(BF16))

        a = a_scr[...]
        colf = colf_scr[...]
        decay = [dec_scr[0], dec_scr[1]]
        nxt_base = pl.multiple_of(jnp.minimum(u + 1, npair - 1) * PAIR, PAIR)
        g_next = jnp.where(u + 1 < npair, gr_ref[:, pl.ds(nxt_base, PAIR)], grn_ref[:, :PAIR])
        nxt = gates(g_next, m_ref[...])

        def colv(c, kind, h, width):
            x = colf[c * CHUNK:(c + 1) * CHUNK, kind * H_B + h:kind * H_B + h + 1]
            return jnp.broadcast_to(x, (CHUNK, width))

        for c in range(2):
            dws, qis, kws = [], [], []
            for h in range(H_B):
                qh, kh = qk[c][h], qk[c][H_B + h]
                a_row = a[h:h + 1, c * CHUNK:(c + 1) * CHUNK]
                d = jnp.where(causal, jnp.exp(a_row - colv(c, 0, h, CHUNK)), 0.0)
                dws.append((d * _dot_nt(qh, kh)).astype(BF16))
                qis.append((qh.astype(F32) * colv(c, 1, h, DK_B)).astype(BF16))
                kws.append((kh.astype(F32) * colv(c, 3, h, DK_B)).astype(BF16))
            for h in range(H_B):
                sl = slice(h * DV_B, (h + 1) * DV_B)
                vaug = jnp.concatenate([v_ref[rows[c], sl], ones_v], axis=1)
                cst = c_ref[h]
                acc2 = _dot(dws[h], vaug) + _dot(qis[h], cst.astype(BF16))
                dec = jnp.broadcast_to(decay[c][h:h + 1, :], (DK_B, DV_B + LANES))
                c_ref[h] = dec * cst + _dot_tn(kws[h], vaug)
                rden = 1.0 / jnp.maximum(jnp.abs(acc2[:, DV_B:]), colv(c, 2, h, LANES))
                hh = acc2[:, :DV_B] * jnp.concatenate([rden, rden], axis=1)
                ms = jnp.mean(hh * hh, axis=-1, keepdims=True)
                gate = _silu_times_sigmoid(z_ref[rows[c], sl].astype(F32),
                                           og_ref[rows[c], sl].astype(F32))
                res = (hh * lax.rsqrt(ms + EPS)) * gn[:, sl] * gate
                o_ref[rows[c], sl] = res.astype(BF16)
        stash(nxt)
        return carry

    @pl.when(pl.program_id(1) == 0)
    def _():
        prev_ref[...] = jnp.zeros_like(prev_ref)
        c_ref[...] = jnp.zeros_like(c_ref)
        stash(gates(gr_ref[:, :PAIR], jnp.zeros((H_B, 1), F32)))

    lax.fori_loop(0, npair, pair, 0)


def _mixer_b(p, grow, conv_w, conv_b, bg, gn_b, batch, seq, tb=512):
    t = p.shape[0]
    nb = seq // tb

    def spec(width, off):
        return pl.BlockSpec((tb, width), lambda b, n, o=off // width: (b * nb + n, o))

    def full(shape):
        return pl.BlockSpec(shape, lambda b, n: (0,) * len(shape))

    return pl.pallas_call(
        functools.partial(_mixer_b_kernel, npair=tb // PAIR),
        grid=(batch, nb),
        in_specs=[spec(2 * D_QK_B, OFF_QB), spec(D_B, OFF_VB),
                  spec(D_B, OFF_OGB), spec(D_B, OFF_ZB),
                  pl.BlockSpec((N_GATES, tb), lambda b, n: (0, b * nb + n)),
                  pl.BlockSpec((N_GATES, tb),
                               lambda b, n: (0, jnp.minimum(b * nb + n + 1, batch * nb - 1))),
                  full((CONV_W, 2 * D_QK_B)), full((1, 2 * D_QK_B)),
                  full((N_GATES, 1)), full((1, D_B))],
        out_specs=pl.BlockSpec((tb, D_B), lambda b, n: (b * nb + n, 0)),
        out_shape=jax.ShapeDtypeStruct((t, D_B), BF16),
        scratch_shapes=[pltpu.VMEM((CHUNK, 2 * D_QK_B), BF16),
                        pltpu.VMEM((H_B, DK_B, DV_B + LANES), F32),
                        pltpu.VMEM((H_B, 1), F32),
                        pltpu.VMEM((H_B, PAIR), F32),
                        pltpu.VMEM((PAIR, 4 * H_B), F32),
                        pltpu.VMEM((2, H_B, 1), F32)],
        compiler_params=_params(("parallel", "arbitrary"), MIXER_FLAGS),
        name="mixer_b",
    )(p, p, p, p, grow, grow, conv_w, conv_b, bg, gn_b)


def _merge_kernel(oa_ref, ob_ref, wa_ref, wb_ref, ga_ref, gb_ref, y_ref):
    ya = _dot(oa_ref[...], wa_ref[...])
    yb = _dot(ob_ref[...], wb_ref[...])
    y = _sigmoid(ga_ref[...].astype(F32)) * ya + _sigmoid(gb_ref[...].astype(F32)) * yb
    y_ref[...] = y.astype(BF16)


def _merge(oa, ob, wa, wb, p, tm=1024, tn=1024):
    t = oa.shape[0]
    return pl.pallas_call(
        _merge_kernel,
        grid=(t // tm, D_MODEL // tn),
        in_specs=[pl.BlockSpec((tm, D_A), lambda i, j: (i, 0)),
                  pl.BlockSpec((tm, D_B), lambda i, j: (i, 0)),
                  pl.BlockSpec((D_A, tn), lambda i, j: (0, j)),
                  pl.BlockSpec((D_B, tn), lambda i, j: (0, j)),
                  pl.BlockSpec((tm, tn), lambda i, j: (i, OFF_GA // tn + j)),
                  pl.BlockSpec((tm, tn), lambda i, j: (i, OFF_GB // tn + j))],
        out_specs=pl.BlockSpec((tm, tn), lambda i, j: (i, j)),
        out_shape=jax.ShapeDtypeStruct((t, D_MODEL), BF16),
        compiler_params=_params(("parallel", "arbitrary")),
        name="merge",
    )(oa, ob, wa, wb, p, p)


def _outproj_kernel(y_ref, w_ref, x_ref, g_ref, o_ref, ss_ref, *, tn):
    j = pl.program_id(1)
    o = _dot(y_ref[...], w_ref[...])
    o_ref[:, pl.ds(pl.multiple_of(j * tn, tn), tn)] = o
    sq = o * o
    part = sq[:, :LANES]
    for c in range(1, tn // LANES):
        part = part + sq[:, c * LANES:(c + 1) * LANES]

    ss_ref[...] = jnp.where(j == 0, part, ss_ref[...] + part)

    @pl.when(j == pl.num_programs(1) - 1)
    def _():
        ms = jnp.sum(ss_ref[...], axis=-1, keepdims=True) * (1.0 / D_MODEL)
        o_ref[...] = x_ref[...] + (o_ref[...] * lax.rsqrt(ms + EPS)) * g_ref[...]


def _outproj(y, w, x2, g, tm=512, tn=512):
    t = y.shape[0]
    return pl.pallas_call(
        functools.partial(_outproj_kernel, tn=tn),
        grid=(t // tm, D_MODEL // tn),
        in_specs=[pl.BlockSpec((tm, D_MODEL), lambda i, j: (i, 0)),
                  pl.BlockSpec((D_MODEL, tn), lambda i, j: (0, j)),
                  pl.BlockSpec((tm, D_MODEL), lambda i, j: (i, 0)),
                  pl.BlockSpec((1, D_MODEL), lambda i, j: (0, 0))],
        out_specs=pl.BlockSpec((tm, D_MODEL), lambda i, j: (i, 0)),
        out_shape=jax.ShapeDtypeStruct((t, D_MODEL), F32),
        scratch_shapes=[pltpu.VMEM((tm, LANES), F32)],
        compiler_params=_params(("parallel", "arbitrary")),
        name="outproj",
    )(y, w, x2, g)


def _layer(x2, batch, seq, layer, g_pre, w_in, lb_logits, conv_w, conv_b, b_ig, b_fg,
           g_norm_a, g_norm_b, w_up_a, w_up_b, w_out, g_post):
    w_in_t = w_in.T
    w_gate = w_in_t[GATE_COL0:GATE_COL0 + N_GATES].astype(BF16)
    bg = jnp.concatenate([b_ig, b_fg]).reshape(N_GATES, 1)

    h, grow = _prenorm(x2, g_pre.reshape(1, D_MODEL), w_gate)
    p = _inproj(h, w_in_t)
    oa = _mixer_a(p, lb_logits, g_norm_a.reshape(1, D_A), layer, batch, seq)
    ob = _mixer_b(p, grow, conv_w, conv_b.reshape(1, 2 * D_QK_B), bg,
                  g_norm_b.reshape(1, D_B), batch, seq)
    y = _merge(oa, ob, w_up_a.astype(BF16), w_up_b.astype(BF16), p)
    return _outproj(y, w_out.astype(BF16), x2, g_post.reshape(1, D_MODEL))


def kernel(x, g_pre, w_in, lb_logits, conv_w, conv_b, b_ig, b_fg, g_norm_a, g_norm_b,
           w_up_a, w_up_b, w_out, g_post):
    batch, seq, _ = x.shape
    depth = g_pre.shape[0]
    x2 = x.reshape(batch * seq, D_MODEL)
    for l in range(depth):
        x2 = _layer(x2, batch, seq, l, g_pre[l], w_in[l], lb_logits, conv_w[l], conv_b[l],
                    b_ig[l], b_fg[l], g_norm_a[l], g_norm_b[l], w_up_a[l], w_up_b[l],
                    w_out[l], g_post[l])
    return x2.reshape(batch, seq, D_MODEL)
```

```python
import functools
import math

import jax
import jax.numpy as jnp
from jax import lax
from jax.experimental import pallas as pl
from jax.experimental.pallas import tpu as pltpu

D_MODEL = 4096
D_A = D_MODEL // 2
HA_DK = 128
H_A = D_A // HA_DK
HA_DV = D_A // H_A
D_B = D_MODEL // 2
H_B = 8
DV_B = D_B // H_B
DK_B = DV_B // 2
D_QK_B = H_B * DK_B
CONV_W = 4
CHUNK = 64
PAIR = 2 * CHUNK
EPS = 1e-6
LANES = 128
SUBLANES = 8

F32 = jnp.float32
BF16 = jnp.bfloat16

OFF_QA, OFF_FA, OFF_IA, OFF_OGA, OFF_ZA = (k * D_A for k in range(5))
OFF_QB = 5 * D_A
OFF_KB = OFF_QB + D_QK_B
OFF_VB = OFF_KB + D_QK_B
OFF_OGB = OFF_VB + D_B
OFF_ZB = OFF_OGB + D_B
OFF_GA = OFF_ZB + D_B
OFF_GB = OFF_GA + D_MODEL
P_COLS = OFF_GB + D_MODEL
GATE_COL0 = OFF_GA
N_GATES = 2 * H_B

VMEM_LIMIT = 60 * 1024 * 1024


def _params(sem, flags=None):
    return pltpu.CompilerParams(dimension_semantics=sem, vmem_limit_bytes=VMEM_LIMIT,
                                flags=flags)


MIXER_FLAGS = None


def _sigmoid(x):
    return 1.0 / (1.0 + jnp.exp(-x))


def _log_sigmoid(x):
    return jnp.minimum(x, 0.0) - jnp.log1p(jnp.exp(-jnp.abs(x)))


def _silu_times_sigmoid(z, og):
    return z / ((1.0 + jnp.exp(-z)) * (1.0 + jnp.exp(-og)))


def _dot(a, b):
    return jnp.dot(a, b, preferred_element_type=F32)


def _dot_nt(a, b):
    return lax.dot_general(a, b, (((1,), (1,)), ((), ())), preferred_element_type=F32)


def _dot_tn(a, b):
    return lax.dot_general(a, b, (((0,), (0,)), ((), ())), preferred_element_type=F32)


def _split_dot(tri, x, left):
    hi = x.astype(BF16)
    lo = (x - hi.astype(F32)).astype(BF16)
    if left:
        return _dot(tri, hi) + _dot(tri, lo)
    return _dot(hi, tri) + _dot(lo, tri)


def _prenorm_kernel(x_ref, g_ref, wg_ref, h_ref, gr_ref):
    x = x_ref[...]
    ms = jnp.mean(x * x, axis=-1, keepdims=True)
    h = ((x * lax.rsqrt(ms + EPS)) * g_ref[...]).astype(BF16)
    h_ref[...] = h
    gr_ref[...] = _dot_nt(wg_ref[...], h)


def _prenorm(x2, g, w_gate, tm=512):
    t = x2.shape[0]
    return pl.pallas_call(
        _prenorm_kernel,
        grid=(t // tm,),
        in_specs=[pl.BlockSpec((tm, D_MODEL), lambda i: (i, 0)),
                  pl.BlockSpec((1, D_MODEL), lambda i: (0, 0)),
                  pl.BlockSpec((N_GATES, D_MODEL), lambda i: (0, 0))],
        out_specs=[pl.BlockSpec((tm, D_MODEL), lambda i: (i, 0)),
                   pl.BlockSpec((N_GATES, tm), lambda i: (0, i))],
        out_shape=[jax.ShapeDtypeStruct((t, D_MODEL), BF16),
                   jax.ShapeDtypeStruct((N_GATES, t), F32)],
        compiler_params=_params(("parallel",)),
        name="prenorm",
    )(x2, g, w_gate)


def _weight_rows(tile, c, *, n_j, tn, rc, first_tail):
    tile = jnp.minimum(tile, n_j - 1)
    skip = jnp.where(tile >= first_tail, N_GATES // SUBLANES, 0)
    return (tile * (tn // SUBLANES) + c * (rc // SUBLANES) + skip) * SUBLANES


def _inproj_kernel(h_ref, wt_hbm, wc_ref, p_ref, wbuf, stage, sem, *, n_i, tn, rows_of):
    j = pl.program_id(0)
    i = pl.program_id(1)
    rc = tn // n_i

    @pl.when(j * n_i + i == 0)
    def _():
        for c in range(n_i):
            cp = pltpu.make_async_copy(wt_hbm.at[pl.ds(rows_of(0, c), rc), :], stage, sem.at[0])
            cp.start()
            cp.wait()
            wbuf[0, c * rc:(c + 1) * rc, :] = stage[...].astype(BF16)

    p_ref[...] = _dot_nt(h_ref[...], wbuf[lax.rem(j, 2)]).astype(BF16)
    wbuf[lax.rem(j + 1, 2), pl.ds(pl.multiple_of(i * rc, rc), rc), :] = wc_ref[...].astype(BF16)


def _inproj(h, w_in_t, tm=1024, tn=1024):
    t, k = h.shape
    n_i, n_j = t // tm, P_COLS // tn
    rc = tn // n_i
    rows_of = functools.partial(_weight_rows, n_j=n_j, tn=tn, rc=rc, first_tail=GATE_COL0 // tn)
    return pl.pallas_call(
        functools.partial(_inproj_kernel, n_i=n_i, tn=tn, rows_of=rows_of),
        grid=(n_j, n_i),
        in_specs=[pl.BlockSpec((tm, k), lambda j, i: (i, 0)),
                  pl.BlockSpec(memory_space=pl.ANY),
                  pl.BlockSpec((pl.Element(rc), pl.Element(k)),
                               lambda j, i: (rows_of(j + 1, i), 0))],
        out_specs=pl.BlockSpec((tm, tn), lambda j, i: (i, j)),
        out_shape=jax.ShapeDtypeStruct((t, P_COLS), BF16),
        scratch_shapes=[pltpu.VMEM((2, tn, k), BF16),
                        pltpu.VMEM((rc, k), F32),
                        pltpu.SemaphoreType.DMA((1,))],
        compiler_params=_params(("arbitrary", "arbitrary")),
        name="inproj",
    )(h, w_in_t, w_in_t)


def _mixer_a_kernel(q_ref, f_ref, i_ref, og_ref, z_ref, lbl_ref, gn_ref, o_ref, st_ref,
                    *, layer, hpb, ngroup, gsz):
    @pl.when(pl.program_id(2) == 0)
    def _():
        st_ref[...] = jnp.zeros_like(st_ref)

    lg = lbl_ref[...]
    e = jnp.exp(lg - jnp.max(lg, axis=0, keepdims=True))
    lb = jnp.sum(e[:layer + 1], axis=0, keepdims=True) / jnp.sum(e, axis=0, keepdims=True)
    gn = gn_ref[...]

    row = lax.broadcasted_iota(jnp.int32, (CHUNK, CHUNK), 0)
    col = lax.broadcasted_iota(jnp.int32, (CHUNK, CHUNK), 1)
    causal = row >= col
    tri = jnp.where(causal, 1.0, 0.0).astype(BF16)
    mid = CHUNK // 2 - 1
    heads = [slice(h * HA_DK, (h + 1) * HA_DK) for h in range(hpb)]

    def group(u, carry):
        rows = [pl.ds(pl.multiple_of((u * gsz + c) * CHUNK, CHUNK), CHUNK) for c in range(gsz)]
        fs = [lb + (1.0 - lb) * _sigmoid(f_ref[r, :].astype(F32)) for r in rows]
        bs = [_split_dot(tri, jnp.log(f), left=True) for f in fs]
        q_rel, k_rel, q_in, k_out, eg = [], [], [], [], []
        for c in range(gsz):
            b = bs[c]
            b_mid = b[mid:mid + 1, :]
            g = b[CHUNK - 1:CHUNK, :]
            e_rel = jnp.exp(b - b_mid)
            qr = q_ref[rows[c], :].astype(F32) * e_rel
            kr = (1.0 - fs[c]) / e_rel
            q_in.append((qr * jnp.exp(b_mid)).astype(BF16))
            k_out.append((kr * jnp.exp(g - b_mid)).astype(BF16))
            q_rel.append(qr.astype(BF16))
            k_rel.append(kr.astype(BF16))
            eg.append(jnp.exp(g))
        vs = [i_ref[r, :] for r in rows]
        scores = [[jnp.where(causal, _dot_nt(q_rel[c][:, sl], k_rel[c][:, sl]), 0.0).astype(BF16)
                   for sl in heads] for c in range(gsz)]
        upd = [[_dot_tn(vs[c][:, sl], k_out[c][:, sl]) for sl in heads] for c in range(gsz)]
        st = [st_ref[h] for h in range(hpb)]
        for c in range(gsz):
            gate = _silu_times_sigmoid(z_ref[rows[c], :].astype(F32),
                                       og_ref[rows[c], :].astype(F32))
            for h, sl in enumerate(heads):
                o = _dot(scores[c][h], vs[c][:, sl]) + _dot_nt(q_in[c][:, sl], st[h].astype(BF16))
                st[h] = st[h] * eg[c][:, sl] + upd[c][h]
                ms = jnp.mean(o * o, axis=-1, keepdims=True)
                res = (o * lax.rsqrt(ms + EPS)) * gn[:, sl] * gate[:, sl]
                o_ref[rows[c], sl] = res.astype(BF16)
        for h in range(hpb):
            st_ref[h] = st[h]
        return carry

    lax.fori_loop(0, ngroup, group, 0)


def _mixer_a(p, lb_logits, gn_a, layer, batch, seq, tb=512, hpb=4, gsz=8):
    t = p.shape[0]
    cw = hpb * HA_DK
    nb = seq // tb
    ng = D_A // cw

    def spec(off):
        return pl.BlockSpec((tb, cw), lambda b, g, n, o=off // cw: (b * nb + n, o + g))

    return pl.pallas_call(
        functools.partial(_mixer_a_kernel, layer=layer, hpb=hpb,
                          ngroup=tb // (CHUNK * gsz), gsz=gsz),
        grid=(batch, ng, nb),
        in_specs=[spec(OFF_QA), spec(OFF_FA), spec(OFF_IA), spec(OFF_OGA), spec(OFF_ZA),
                  pl.BlockSpec((lb_logits.shape[0], cw), lambda b, g, n: (0, g)),
                  pl.BlockSpec((1, cw), lambda b, g, n: (0, g))],
        out_specs=pl.BlockSpec((tb, cw), lambda b, g, n: (b * nb + n, g)),
        out_shape=jax.ShapeDtypeStruct((t, D_A), BF16),
        scratch_shapes=[pltpu.VMEM((hpb, HA_DV, HA_DK), F32)],
        compiler_params=_params(("parallel", "parallel", "arbitrary"), MIXER_FLAGS),
        name="mixer_a",
    )(p, p, p, p, p, lb_logits, gn_a)


def _mixer_b_kernel(qk_ref, v_ref, og_ref, z_ref, gr_ref, grn_ref, cw_ref, cb_ref, bg_ref, gn_ref,
                    o_ref, prev_ref, c_ref, m_ref, a_scr, colf_scr, dec_scr, *, npair):

    r_sh = lax.broadcasted_iota(jnp.int32, (CHUNK, PAIR), 0)
    c_sh = lax.broadcasted_iota(jnp.int32, (CHUNK, PAIR), 1)
    shift = jnp.concatenate(
        [jnp.where(c_sh == r_sh + (CHUNK - s), 1.0, 0.0) for s in range(1, CONV_W)],
        axis=0).astype(BF16)
    row = lax.broadcasted_iota(jnp.int32, (CHUNK, CHUNK), 0)
    col = lax.broadcasted_iota(jnp.int32, (CHUNK, CHUNK), 1)
    causal = row >= col
    def lane_of(i):
        return jnp.where(i >= CHUNK, i - CHUNK, i)

    r_p = lax.broadcasted_iota(jnp.int32, (PAIR, PAIR), 0)
    c_p = lax.broadcasted_iota(jnp.int32, (PAIR, PAIR), 1)
    tri2 = jnp.where(r_p <= c_p, jnp.where(c_p - r_p <= lane_of(c_p), 1.0, 0.0),
                     0.0).astype(BF16)
    lane = lax.broadcasted_iota(jnp.int32, (H_B, PAIR), 1)
    second = lane >= CHUNK
    lane_in_chunk = lane_of(lane)
    ones_v = jnp.ones((CHUNK, LANES), BF16)
    cw = cw_ref[...]
    cb = cb_ref[...]
    gn = gn_ref[...]
    bg = bg_ref[...]
    log_kscale = -0.5 * math.log(DK_B)

    def gates(g, m_prev):
        g = g + bg
        b = _split_dot(tri2, _log_sigmoid(g[H_B:]), left=False)
        a = g[:H_B] - b
        cm = a
        for sft in (1, 2, 4, 8, 16, 32):
            cm = jnp.maximum(cm, jnp.where(lane_in_chunk >= sft,
                                           pltpu.roll(cm, sft, axis=1), -jnp.inf))
        big_m_a_end = jnp.maximum(cm[:, CHUNK - 1:CHUNK], m_prev)
        m_mid = b[:, CHUNK - 1:CHUNK] + big_m_a_end
        m_start = jnp.where(second, m_mid, m_prev)
        big_m = jnp.maximum(cm, m_start)
        big_m_b_end = big_m[:, PAIR - 1:PAIR]
        big_m_end = jnp.where(second, big_m_b_end, big_m_a_end)
        inter = jnp.exp(m_start - big_m)
        enm = jnp.exp(-(b + big_m))
        a_k = a + log_kscale
        w = jnp.exp(a_k - big_m_end)
        colf = jnp.concatenate([big_m, inter, enm, w], axis=0).T
        return (a_k, colf, jnp.exp(m_prev - big_m_a_end), jnp.exp(m_mid - big_m_b_end),
                b[:, PAIR - 1:PAIR] + big_m_b_end)

    def stash(gq):
        a_scr[...] = gq[0]
        colf_scr[...] = gq[1]
        dec_scr[0] = gq[2]
        dec_scr[1] = gq[3]
        m_ref[...] = gq[4]

    def pair(u, carry):
        base = pl.multiple_of(u * PAIR, PAIR)
        rows = [pl.ds(pl.multiple_of(u * PAIR + c * CHUNK, CHUNK), CHUNK) for c in range(2)]

        cur = qk_ref[pl.ds(base, PAIR), :]
        windows = [jnp.concatenate([prev_ref[...], cur[:CHUNK]], axis=0), cur]
        prev_ref[...] = cur[CHUNK:]
        qk = []
        for c in range(2):
            qk.append([])
            for j in range(2 * H_B):
                ls = slice(j * DK_B, (j + 1) * DK_B)
                sh = _dot(shift, windows[c][:, ls])
                acc = cb[:, ls] + cw[CONV_W - 1:CONV_W, ls] * windows[c][CHUNK:, ls].astype(F32)
                for s in range(1, CONV_W):
                    acc = acc + cw[CONV_W - 1 - s:CONV_W - s, ls] * sh[(s - 1) * CHUNK:s * CHUNK]
                act = acc / (1.0 + jnp.exp(-acc))
                qk[c].append(act.astype(BF16))

        a = a_scr[...]
        colf = colf_scr[...]
        decay = [dec_scr[0], dec_scr[1]]
        nxt_base = pl.multiple_of(jnp.minimum(u + 1, npair - 1) * PAIR, PAIR)
        g_next = jnp.where(u + 1 < npair, gr_ref[:, pl.ds(nxt_base, PAIR)], grn_ref[:, :PAIR])
        nxt = gates(g_next, m_ref[...])

        def colv(c, kind, h, width):
            x = colf[c * CHUNK:(c + 1) * CHUNK, kind * H_B + h:kind * H_B + h + 1]
            return jnp.broadcast_to(x, (CHUNK, width))

        for c in range(2):
            dws, qis, kws = [], [], []
            for h in range(H_B):
                qh, kh = qk[c][h], qk[c][H_B + h]
                a_row = a[h:h + 1, c * CHUNK:(c + 1) * CHUNK]
                d = jnp.where(causal, jnp.exp(a_row - colv(c, 0, h, CHUNK)), 0.0)
                dws.append((d * _dot_nt(qh, kh)).astype(BF16))
                qis.append((qh.astype(F32) * colv(c, 1, h, DK_B)).astype(BF16))
                kws.append((kh.astype(F32) * colv(c, 3, h, DK_B)).astype(BF16))
            for h in range(H_B):
                sl = slice(h * DV_B, (h + 1) * DV_B)
                vaug = jnp.concatenate([v_ref[rows[c], sl], ones_v], axis=1)
                cst = c_ref[h]
                acc2 = _dot(dws[h], vaug) + _dot(qis[h], cst.astype(BF16))
                dec = jnp.broadcast_to(decay[c][h:h + 1, :], (DK_B, DV_B + LANES))
                c_ref[h] = dec * cst + _dot_tn(kws[h], vaug)
                rden = 1.0 / jnp.maximum(jnp.abs(acc2[:, DV_B:]), colv(c, 2, h, LANES))
                hh = acc2[:, :DV_B] * jnp.concatenate([rden, rden], axis=1)
                ms = jnp.mean(hh * hh, axis=-1, keepdims=True)
                gate = _silu_times_sigmoid(z_ref[rows[c], sl].astype(F32),
                                           og_ref[rows[c], sl].astype(F32))
                res = (hh * lax.rsqrt(ms + EPS)) * gn[:, sl] * gate
                o_ref[rows[c], sl] = res.astype(BF16)
        stash(nxt)
        return carry

    @pl.when(pl.program_id(1) == 0)
    def _():
        prev_ref[...] = jnp.zeros_like(prev_ref)
        c_ref[...] = jnp.zeros_like(c_ref)
        stash(gates(gr_ref[:, :PAIR], jnp.zeros((H_B, 1), F32)))

    lax.fori_loop(0, npair, pair, 0)


def _mixer_b(p, grow, conv_w, conv_b, bg, gn_b, batch, seq, tb=512):
    t = p.shape[0]
    nb = seq // tb

    def spec(width, off):
        return pl.BlockSpec((tb, width), lambda b, n, o=off // width: (b * nb + n, o))

    def full(shape):
        return pl.BlockSpec(shape, lambda b, n: (0,) * len(shape))

    return pl.pallas_call(
        functools.partial(_mixer_b_kernel, npair=tb // PAIR),
        grid=(batch, nb),
        in_specs=[spec(2 * D_QK_B, OFF_QB), spec(D_B, OFF_VB),
                  spec(D_B, OFF_OGB), spec(D_B, OFF_ZB),
                  pl.BlockSpec((N_GATES, tb), lambda b, n: (0, b * nb + n)),
                  pl.BlockSpec((N_GATES, tb),
                               lambda b, n: (0, jnp.minimum(b * nb + n + 1, batch * nb - 1))),
                  full((CONV_W, 2 * D_QK_B)), full((1, 2 * D_QK_B)),
                  full((N_GATES, 1)), full((1, D_B))],
        out_specs=pl.BlockSpec((tb, D_B), lambda b, n: (b * nb + n, 0)),
        out_shape=jax.ShapeDtypeStruct((t, D_B), BF16),
        scratch_shapes=[pltpu.VMEM((CHUNK, 2 * D_QK_B), BF16),
                        pltpu.VMEM((H_B, DK_B, DV_B + LANES), F32),
                        pltpu.VMEM((H_B, 1), F32),
                        pltpu.VMEM((H_B, PAIR), F32),
                        pltpu.VMEM((PAIR, 4 * H_B), F32),
                        pltpu.VMEM((2, H_B, 1), F32)],
        compiler_params=_params(("parallel", "arbitrary"), MIXER_FLAGS),
        name="mixer_b",
    )(p, p, p, p, grow, grow, conv_w, conv_b, bg, gn_b)


def _merge_kernel(oa_ref, ob_ref, wa_ref, wb_ref, ga_ref, gb_ref, y_ref):
    ya = _dot(oa_ref[...], wa_ref[...])
    yb = _dot(ob_ref[...], wb_ref[...])
    y = _sigmoid(ga_ref[...].astype(F32)) * ya + _sigmoid(gb_ref[...].astype(F32)) * yb
    y_ref[...] = y.astype(BF16)


def _merge(oa, ob, wa, wb, p, tm=1024, tn=1024):
    t = oa.shape[0]
    return pl.pallas_call(
        _merge_kernel,
        grid=(t // tm, D_MODEL // tn),
        in_specs=[pl.BlockSpec((tm, D_A), lambda i, j: (i, 0)),
                  pl.BlockSpec((tm, D_B), lambda i, j: (i, 0)),
                  pl.BlockSpec((D_A, tn), lambda i, j: (0, j)),
                  pl.BlockSpec((D_B, tn), lambda i, j: (0, j)),
                  pl.BlockSpec((tm, tn), lambda i, j: (i, OFF_GA // tn + j)),
                  pl.BlockSpec((tm, tn), lambda i, j: (i, OFF_GB // tn + j))],
        out_specs=pl.BlockSpec((tm, tn), lambda i, j: (i, j)),
        out_shape=jax.ShapeDtypeStruct((t, D_MODEL), BF16),
        compiler_params=_params(("parallel", "arbitrary")),
        name="merge",
    )(oa, ob, wa, wb, p, p)


def _outproj_kernel(y_ref, w_ref, x_ref, g_ref, o_ref, ss_ref, *, tn):
    j = pl.program_id(1)
    o = _dot(y_ref[...], w_ref[...])
    o_ref[:, pl.ds(pl.multiple_of(j * tn, tn), tn)] = o
    sq = o * o
    part = sq[:, :LANES]
    for c in range(1, tn // LANES):
        part = part + sq[:, c * LANES:(c + 1) * LANES]

    ss_ref[...] = jnp.where(j == 0, part, ss_ref[...] + part)

    @pl.when(j == pl.num_programs(1) - 1)
    def _():
        ms = jnp.sum(ss_ref[...], axis=-1, keepdims=True) * (1.0 / D_MODEL)
        o_ref[...] = x_ref[...] + (o_ref[...] * lax.rsqrt(ms + EPS)) * g_ref[...]


def _outproj(y, w, x2, g, tm=512, tn=512):
    t = y.shape[0]
    return pl.pallas_call(
        functools.partial(_outproj_kernel, tn=tn),
        grid=(t // tm, D_MODEL // tn),
        in_specs=[pl.BlockSpec((tm, D_MODEL), lambda i, j: (i, 0)),
                  pl.BlockSpec((D_MODEL, tn), lambda i, j: (0, j)),
                  pl.BlockSpec((tm, D_MODEL), lambda i, j: (i, 0)),
                  pl.BlockSpec((1, D_MODEL), lambda i, j: (0, 0))],
        out_specs=pl.BlockSpec((tm, D_MODEL), lambda i, j: (i, 0)),
        out_shape=jax.ShapeDtypeStruct((t, D_MODEL), F32),
        scratch_shapes=[pltpu.VMEM((tm, LANES), F32)],
        compiler_params=_params(("parallel", "arbitrary")),
        name="outproj",
    )(y, w, x2, g)


def _layer(x2, batch, seq, layer, g_pre, w_in, lb_logits, conv_w, conv_b, b_ig, b_fg,
           g_norm_a, g_norm_b, w_up_a, w_up_b, w_out, g_post):
    w_in_t = w_in.T
    w_gate = w_in_t[GATE_COL0:GATE_COL0 + N_GATES].astype(BF16)
    bg = jnp.concatenate([b_ig, b_fg]).reshape(N_GATES, 1)

    h, grow = _prenorm(x2, g_pre.reshape(1, D_MODEL), w_gate)
    p = _inproj(h, w_in_t)
    oa = _mixer_a(p, lb_logits, g_norm_a.reshape(1, D_A), layer, batch, seq)
    ob = _mixer_b(p, grow, conv_w, conv_b.reshape(1, 2 * D_QK_B), bg,
                  g_norm_b.reshape(1, D_B), batch, seq)
    y = _merge(oa, ob, w_up_a.astype(BF16), w_up_b.astype(BF16), p)
    return _outproj(y, w_out.astype(BF16), x2, g_post.reshape(1, D_MODEL))


def kernel(x, g_pre, w_in, lb_logits, conv_w, conv_b, b_ig, b_fg, g_norm_a, g_norm_b,
           w_up_a, w_up_b, w_out, g_post):
    batch, seq, _ = x.shape
    depth = g_pre.shape[0]
    x2 = x.reshape(batch * seq, D_MODEL)
    for l in range(depth):
        x2 = _layer(x2, batch, seq, l, g_pre[l], w_in[l], lb_logits, conv_w[l], conv_b[l],
                    b_ig[l], b_fg[l], g_norm_a[l], g_norm_b[l], w_up_a[l], w_up_b[l],
                    w_out[l], g_post[l])
    return x2.reshape(batch, seq, D_MODEL)
```

```python
import functools
import math

import jax
import jax.numpy as jnp
from jax import lax
from jax.experimental import pallas as pl
from jax.experimental.pallas import tpu as pltpu

D_MODEL = 4096
D_A = D_MODEL // 2
HA_DK = 128
H_A = D_A // HA_DK
HA_DV = D_A // H_A
D_B = D_MODEL // 2
H_B = 8
DV_B = D_B // H_B
DK_B = DV_B // 2
D_QK_B = H_B * DK_B
CONV_W = 4
CHUNK = 64
PAIR = 2 * CHUNK
EPS = 1e-6
LANES = 128
SUBLANES = 8

F32 = jnp.float32
BF16 = jnp.bfloat16

OFF_QA, OFF_FA, OFF_IA, OFF_OGA, OFF_ZA = (k * D_A for k in range(5))
OFF_QB = 5 * D_A
OFF_KB = OFF_QB + D_QK_B
OFF_VB = OFF_KB + D_QK_B
OFF_OGB = OFF_VB + D_B
OFF_ZB = OFF_OGB + D_B
OFF_GA = OFF_ZB + D_B
OFF_GB = OFF_GA + D_MODEL
P_COLS = OFF_GB + D_MODEL
GATE_COL0 = OFF_GA
N_GATES = 2 * H_B

VMEM_LIMIT = 60 * 1024 * 1024


def _params(sem, flags=None):
    return pltpu.CompilerParams(dimension_semantics=sem, vmem_limit_bytes=VMEM_LIMIT,
                                flags=flags)


MIXER_FLAGS = None


def _sigmoid(x):
    return 1.0 / (1.0 + jnp.exp(-x))


def _log_sigmoid(x):
    return jnp.minimum(x, 0.0) - jnp.log1p(jnp.exp(-jnp.abs(x)))


def _silu_times_sigmoid(z, og):
    return z / ((1.0 + jnp.exp(-z)) * (1.0 + jnp.exp(-og)))


def _dot(a, b):
    return jnp.dot(a, b, preferred_element_type=F32)


def _dot_nt(a, b):
    return lax.dot_general(a, b, (((1,), (1,)), ((), ())), preferred_element_type=F32)


def _dot_tn(a, b):
    return lax.dot_general(a, b, (((0,), (0,)), ((), ())), preferred_element_type=F32)


def _split_dot(tri, x, left):
    hi = x.astype(BF16)
    lo = (x - hi.astype(F32)).astype(BF16)
    if left:
        return _dot(tri, hi) + _dot(tri, lo)
    return _dot(hi, tri) + _dot(lo, tri)


def _prenorm_kernel(x_ref, g_ref, wg_ref, h_ref, gr_ref):
    x = x_ref[...]
    ms = jnp.mean(x * x, axis=-1, keepdims=True)
    h = ((x * lax.rsqrt(ms + EPS)) * g_ref[...]).astype(BF16)
    h_ref[...] = h
    gr_ref[...] = _dot_nt(wg_ref[...], h)


def _prenorm(x2, g, w_gate, tm=512):
    t = x2.shape[0]
    return pl.pallas_call(
        _prenorm_kernel,
        grid=(t // tm,),
        in_specs=[pl.BlockSpec((tm, D_MODEL), lambda i: (i, 0)),
                  pl.BlockSpec((1, D_MODEL), lambda i: (0, 0)),
                  pl.BlockSpec((N_GATES, D_MODEL), lambda i: (0, 0))],
        out_specs=[pl.BlockSpec((tm, D_MODEL), lambda i: (i, 0)),
                   pl.BlockSpec((N_GATES, tm), lambda i: (0, i))],
        out_shape=[jax.ShapeDtypeStruct((t, D_MODEL), BF16),
                   jax.ShapeDtypeStruct((N_GATES, t), F32)],
        compiler_params=_params(("parallel",)),
        name="prenorm",
    )(x2, g, w_gate)


def _weight_rows(tile, c, *, n_j, tn, rc, first_tail):
    tile = jnp.minimum(tile, n_j - 1)
    skip = jnp.where(tile >= first_tail, N_GATES // SUBLANES, 0)
    return (tile * (tn // SUBLANES) + c * (rc // SUBLANES) + skip) * SUBLANES


def _inproj_kernel(h_ref, wt_hbm, wc_ref, p_ref, wbuf, stage, sem, *, n_i, tn, rows_of):
    j = pl.program_id(0)
    i = pl.program_id(1)
    rc = tn // n_i

    @pl.when(j * n_i + i == 0)
    def _():
        for c in range(n_i):
            cp = pltpu.make_async_copy(wt_hbm.at[pl.ds(rows_of(0, c), rc), :], stage, sem.at[0])
            cp.start()
            cp.wait()
            wbuf[0, c * rc:(c + 1) * rc, :] = stage[...].astype(BF16)

    p_ref[...] = _dot_nt(h_ref[...], wbuf[lax.rem(j, 2)]).astype(BF16)
    wbuf[lax.rem(j + 1, 2), pl.ds(pl.multiple_of(i * rc, rc), rc), :] = wc_ref[...].astype(BF16)


def _inproj(h, w_in_t, tm=1024, tn=1024):
    t, k = h.shape
    n_i, n_j = t // tm, P_COLS // tn
    rc = tn // n_i
    rows_of = functools.partial(_weight_rows, n_j=n_j, tn=tn, rc=rc, first_tail=GATE_COL0 // tn)
    return pl.pallas_call(
        functools.partial(_inproj_kernel, n_i=n_i, tn=tn, rows_of=rows_of),
        grid=(n_j, n_i),
        in_specs=[pl.BlockSpec((tm, k), lambda j, i: (i, 0)),
                  pl.BlockSpec(memory_space=pl.ANY),
                  pl.BlockSpec((pl.Element(rc), pl.Element(k)),
                               lambda j, i: (rows_of(j + 1, i), 0))],
        out_specs=pl.BlockSpec((tm, tn), lambda j, i: (i, j)),
        out_shape=jax.ShapeDtypeStruct((t, P_COLS), BF16),
        scratch_shapes=[pltpu.VMEM((2, tn, k), BF16),
                        pltpu.VMEM((rc, k), F32),
                        pltpu.SemaphoreType.DMA((1,))],
        compiler_params=_params(("arbitrary", "arbitrary")),
        name="inproj",
    )(h, w_in_t, w_in_t)


def _mixer_a_kernel(q_ref, f_ref, i_ref, og_ref, z_ref, lbl_ref, gn_ref, o_ref, st_ref,
                    *, layer, hpb, ngroup, gsz):
    @pl.when(pl.program_id(2) == 0)
    def _():
        st_ref[...] = jnp.zeros_like(st_ref)

    lg = lbl_ref[...]
    e = jnp.exp(lg - jnp.max(lg, axis=0, keepdims=True))
    lb = jnp.sum(e[:layer + 1], axis=0, keepdims=True) / jnp.sum(e, axis=0, keepdims=True)
    gn = gn_ref[...]

    row = lax.broadcasted_iota(jnp.int32, (CHUNK, CHUNK), 0)
    col = lax.broadcasted_iota(jnp.int32, (CHUNK, CHUNK), 1)
    causal = row >= col
    tri = jnp.where(causal, 1.0, 0.0).astype(BF16)
    mid = CHUNK // 2 - 1
    heads = [slice(h * HA_DK, (h + 1) * HA_DK) for h in range(hpb)]

    def group(u, carry):
        rows = [pl.ds(pl.multiple_of((u * gsz + c) * CHUNK, CHUNK), CHUNK) for c in range(gsz)]
        fs = [lb + (1.0 - lb) * _sigmoid(f_ref[r, :].astype(F32)) for r in rows]
        bs = [_split_dot(tri, jnp.log(f), left=True) for f in fs]
        q_rel, k_rel, q_in, k_out, eg = [], [], [], [], []
        for c in range(gsz):
            b = bs[c]
            b_mid = b[mid:mid + 1, :]
            g = b[CHUNK - 1:CHUNK, :]
            e_rel = jnp.exp(b - b_mid)
            qr = q_ref[rows[c], :].astype(F32) * e_rel
            kr = (1.0 - fs[c]) / e_rel
            q_in.append((qr * jnp.exp(b_mid)).astype(BF16))
            k_out.append((kr * jnp.exp(g - b_mid)).astype(BF16))
            q_rel.append(qr.astype(BF16))
            k_rel.append(kr.astype(BF16))
            eg.append(jnp.exp(g))
        vs = [i_ref[r, :] for r in rows]
        scores = [[jnp.where(causal, _dot_nt(q_rel[c][:, sl], k_rel[c][:, sl]), 0.0).astype(BF16)
                   for sl in heads] for c in range(gsz)]
        upd = [[_dot_tn(vs[c][:, sl], k_out[c][:, sl]) for sl in heads] for c in range(gsz)]
        st = [st_ref[h] for h in range(hpb)]
        for c in range(gsz):
            gate = _silu_times_sigmoid(z_ref[rows[c], :].astype(F32),
                                       og_ref[rows[c], :].astype(F32))
            for h, sl in enumerate(heads):
                o = _dot(scores[c][h], vs[c][:, sl]) + _dot_nt(q_in[c][:, sl], st[h].astype(BF16))
                st[h] = st[h] * eg[c][:, sl] + upd[c][h]
                ms = jnp.mean(o * o, axis=-1, keepdims=True)
                res = (o * lax.rsqrt(ms + EPS)) * gn[:, sl] * gate[:, sl]
                o_ref[rows[c], sl] = res.astype(BF16)
        for h in range(hpb):
            st_ref[h] = st[h]
        return carry

    lax.fori_loop(0, ngroup, group, 0)


def _mixer_a(p, lb_logits, gn_a, layer, batch, seq, tb=512, hpb=4, gsz=8):
    t = p.shape[0]
    cw = hpb * HA_DK
    nb = seq // tb
    ng = D_A // cw

    def spec(off):
        return pl.BlockSpec((tb, cw), lambda b, g, n, o=off // cw: (b * nb + n, o + g))

    return pl.pallas_call(
        functools.partial(_mixer_a_kernel, layer=layer, hpb=hpb,
                          ngroup=tb // (CHUNK * gsz), gsz=gsz),
        grid=(batch, ng, nb),
        in_specs=[spec(OFF_QA), spec(OFF_FA), spec(OFF_IA), spec(OFF_OGA), spec(OFF_ZA),
                  pl.BlockSpec((lb_logits.shape[0], cw), lambda b, g, n: (0, g)),
                  pl.BlockSpec((1, cw), lambda b, g, n: (0, g))],
        out_specs=pl.BlockSpec((tb, cw), lambda b, g, n: (b * nb + n, g)),
        out_shape=jax.ShapeDtypeStruct((t, D_A), BF16),
        scratch_shapes=[pltpu.VMEM((hpb, HA_DV, HA_DK), F32)],
        compiler_params=_params(("parallel", "parallel", "arbitrary"), MIXER_FLAGS),
        name="mixer_a",
    )(p, p, p, p, p, lb_logits, gn_a)


def _mixer_b_kernel(qk_ref, v_ref, og_ref, z_ref, gr_ref, grn_ref, cw_ref, cb_ref, bg_ref, gn_ref,
                    o_ref, prev_ref, c_ref, m_ref, a_scr, colf_scr, dec_scr, *, npair):

    r_sh = lax.broadcasted_iota(jnp.int32, (CHUNK, PAIR), 0)
    c_sh = lax.broadcasted_iota(jnp.int32, (CHUNK, PAIR), 1)
    shift = jnp.concatenate(
        [jnp.where(c_sh == r_sh + (CHUNK - s), 1.0, 0.0) for s in range(1, CONV_W)],
        axis=0).astype(BF16)
    row = lax.broadcasted_iota(jnp.int32, (CHUNK, CHUNK), 0)
    col = lax.broadcasted_iota(jnp.int32, (CHUNK, CHUNK), 1)
    causal = row >= col
    def lane_of(i):
        return jnp.where(i >= CHUNK, i - CHUNK, i)

    r_p = lax.broadcasted_iota(jnp.int32, (PAIR, PAIR), 0)
    c_p = lax.broadcasted_iota(jnp.int32, (PAIR, PAIR), 1)
    tri2 = jnp.where(r_p <= c_p, jnp.where(c_p - r_p <= lane_of(c_p), 1.0, 0.0),
                     0.0).astype(BF16)
    lane = lax.broadcasted_iota(jnp.int32, (H_B, PAIR), 1)
    second = lane >= CHUNK
    lane_in_chunk = lane_of(lane)
    ones_v = jnp.ones((CHUNK, LANES), BF16)
    cw = cw_ref[...]
    cb = cb_ref[...]
    gn = gn_ref[...]
    bg = bg_ref[...]
    log_kscale = -0.5 * math.log(DK_B)

    def gates(g, m_prev):
        g = g + bg
        b = _split_dot(tri2, _log_sigmoid(g[H_B:]), left=False)
        a = g[:H_B] - b
        cm = a
        for sft in (1, 2, 4, 8, 16, 32):
            cm = jnp.maximum(cm, jnp.where(lane_in_chunk >= sft,
                                           pltpu.roll(cm, sft, axis=1), -jnp.inf))
        big_m_a_end = jnp.maximum(cm[:, CHUNK - 1:CHUNK], m_prev)
        m_mid = b[:, CHUNK - 1:CHUNK] + big_m_a_end
        m_start = jnp.where(second, m_mid, m_prev)
        big_m = jnp.maximum(cm, m_start)
        big_m_b_end = big_m[:, PAIR - 1:PAIR]
        big_m_end = jnp.where(second, big_m_b_end, big_m_a_end)
        inter = jnp.exp(m_start - big_m)
        enm = jnp.exp(-(b + big_m))
        a_k = a + log_kscale
        w = jnp.exp(a_k - big_m_end)
        colf = jnp.concatenate([big_m, inter, enm, w], axis=0).T
        return (a_k, colf, jnp.exp(m_prev - big_m_a_end), jnp.exp(m_mid - big_m_b_end),
                b[:, PAIR - 1:PAIR] + big_m_b_end)

    def stash(gq):
        a_scr[...] = gq[0]
        colf_scr[...] = gq[1]
        dec_scr[0] = gq[2]
        dec_scr[1] = gq[3]
        m_ref[...] = gq[4]

    def pair(u, carry):
        base = pl.multiple_of(u * PAIR, PAIR)
        rows = [pl.ds(pl.multiple_of(u * PAIR + c * CHUNK, CHUNK), CHUNK) for c in range(2)]

        cur = qk_ref[pl.ds(base, PAIR), :]
        windows = [jnp.concatenate([prev_ref[...], cur[:CHUNK]], axis=0), cur]
        prev_ref[...] = cur[CHUNK:]
        qk = []
        for c in range(2):
            qk.append([])
            for j in range(2 * H_B):
                ls = slice(j * DK_B, (j + 1) * DK_B)
                sh = _dot(shift, windows[c][:, ls])
                acc = cb[:, ls] + cw[CONV_W - 1:CONV_W, ls] * windows[c][CHUNK:, ls].astype(F32)
                for s in range(1, CONV_W):
                    acc = acc + cw[CONV_W - 1 - s:CONV_W - s, ls] * sh[(s - 1) * CHUNK:s * CHUNK]
                act = acc / (1.0 + jnp.exp(-acc))
                qk[c].append(act.astype(BF16))

        a = a_scr[...]
        colf = colf_scr[...]
        decay = [dec_scr[0], dec_scr[1]]
        nxt_base = pl.multiple_of(jnp.minimum(u + 1, npair - 1) * PAIR, PAIR)
        g_next = jnp.where(u + 1 < npair, gr_ref[:, pl.ds(nxt_base, PAIR)], grn_ref[:, :PAIR])
        nxt = gates(g_next, m_ref[...])

        def colv(c, kind, h, width):
            x = colf[c * CHUNK:(c + 1) * CHUNK, kind * H_B + h:kind * H_B + h + 1]
            return jnp.broadcast_to(x, (CHUNK, width))

        for c in range(2):
            dws, qis, kws = [], [], []
            for h in range(H_B):
                qh, kh = qk[c][h], qk[c][H_B + h]
                a_row = a[h:h + 1, c * CHUNK:(c + 1) * CHUNK]
                d = jnp.where(causal, jnp.exp(a_row - colv(c, 0, h, CHUNK)), 0.0)
                dws.append((d * _dot_nt(qh, kh)).astype(BF16))
                qis.append((qh.astype(F32) * colv(c, 1, h, DK_B)).astype(BF16))
                kws.append((kh.astype(F32) * colv(c, 3, h, DK_B)).astype(BF16))
            for h in range(H_B):
                sl = slice(h * DV_B, (h + 1) * DV_B)
                vaug = jnp.concatenate([v_ref[rows[c], sl], ones_v], axis=1)
                cst = c_ref[h]
                acc2 = _dot(dws[h], vaug) + _dot(qis[h], cst.astype(BF16))
                dec = jnp.broadcast_to(decay[c][h:h + 1, :], (DK_B, DV_B + LANES))
                c_ref[h] = dec * cst + _dot_tn(kws[h], vaug)
                rden = 1.0 / jnp.maximum(jnp.abs(acc2[:, DV_B:]), colv(c, 2, h, LANES))
                hh = acc2[:, :DV_B] * jnp.concatenate([rden, rden], axis=1)
                ms = jnp.mean(hh * hh, axis=-1, keepdims=True)
                gate = _silu_times_sigmoid(z_ref[rows[c], sl].astype(F32),
                                           og_ref[rows[c], sl].astype(F32))
                res = (hh * lax.rsqrt(ms + EPS)) * gn[:, sl] * gate
                o_ref[rows[c], sl] = res.astype(BF16)
        stash(nxt)
        return carry

    @pl.when(pl.program_id(1) == 0)
    def _():
        prev_ref[...] = jnp.zeros_like(prev_ref)
        c_ref[...] = jnp.zeros_like(c_ref)
        stash(gates(gr_ref[:, :PAIR], jnp.zeros((H_B, 1), F32)))

    lax.fori_loop(0, npair, pair, 0)


def _mixer_b(p, grow, conv_w, conv_b, bg, gn_b, batch, seq, tb=512):
    t = p.shape[0]
    nb = seq // tb

    def spec(width, off):
        return pl.BlockSpec((tb, width), lambda b, n, o=off // width: (b * nb + n, o))

    def full(shape):
        return pl.BlockSpec(shape, lambda b, n: (0,) * len(shape))

    return pl.pallas_call(
        functools.partial(_mixer_b_kernel, npair=tb // PAIR),
        grid=(batch, nb),
        in_specs=[spec(2 * D_QK_B, OFF_QB), spec(D_B, OFF_VB),
                  spec(D_B, OFF_OGB), spec(D_B, OFF_ZB),
                  pl.BlockSpec((N_GATES, tb), lambda b, n: (0, b * nb + n)),
                  pl.BlockSpec((N_GATES, tb),
                               lambda b, n: (0, jnp.minimum(b * nb + n + 1, batch * nb - 1))),
                  full((CONV_W, 2 * D_QK_B)), full((1, 2 * D_QK_B)),
                  full((N_GATES, 1)), full((1, D_B))],
        out_specs=pl.BlockSpec((tb, D_B), lambda b, n: (b * nb + n, 0)),
        out_shape=jax.ShapeDtypeStruct((t, D_B), BF16),
        scratch_shapes=[pltpu.VMEM((CHUNK, 2 * D_QK_B), BF16),
                        pltpu.VMEM((H_B, DK_B, DV_B + LANES), F32),
                        pltpu.VMEM((H_B, 1), F32),
                        pltpu.VMEM((H_B, PAIR), F32),
                        pltpu.VMEM((PAIR, 4 * H_B), F32),
                        pltpu.VMEM((2, H_B, 1), F32)],
        compiler_params=_params(("parallel", "arbitrary"), MIXER_FLAGS),
        name="mixer_b",
    )(p, p, p, p, grow, grow, conv_w, conv_b, bg, gn_b)


def _merge_kernel(oa_ref, ob_ref, wa_ref, wb_ref, ga_ref, gb_ref, y_ref):
    ya = _dot(oa_ref[...], wa_ref[...])
    yb = _dot(ob_ref[...], wb_ref[...])
    y = _sigmoid(ga_ref[...].astype(F32)) * ya + _sigmoid(gb_ref[...].astype(F32)) * yb
    y_ref[...] = y.astype(BF16)


def _merge(oa, ob, wa, wb, p, tm=1024, tn=1024):
    t = oa.shape[0]
    return pl.pallas_call(
        _merge_kernel,
        grid=(t // tm, D_MODEL // tn),
        in_specs=[pl.BlockSpec((tm, D_A), lambda i, j: (i, 0)),
                  pl.BlockSpec((tm, D_B), lambda i, j: (i, 0)),
                  pl.BlockSpec((D_A, tn), lambda i, j: (0, j)),
                  pl.BlockSpec((D_B, tn), lambda i, j: (0, j)),
                  pl.BlockSpec((tm, tn), lambda i, j: (i, OFF_GA // tn + j)),
                  pl.BlockSpec((tm, tn), lambda i, j: (i, OFF_GB // tn + j))],
        out_specs=pl.BlockSpec((tm, tn), lambda i, j: (i, j)),
        out_shape=jax.ShapeDtypeStruct((t, D_MODEL), BF16),
        compiler_params=_params(("parallel", "arbitrary")),
        name="merge",
    )(oa, ob, wa, wb, p, p)


def _outproj_kernel(y_ref, w_ref, x_ref, g_ref, o_ref):
    o = _dot(y_ref[...], w_ref[...])
    ms = jnp.mean(o * o, axis=-1, keepdims=True)
    o_ref[...] = x_ref[...] + (o * lax.rsqrt(ms + EPS)) * g_ref[...]


def _outproj(y, w, x2, g, tm=256):
    t = y.shape[0]
    return pl.pallas_call(
        _outproj_kernel,
        grid=(t // tm,),
        in_specs=[pl.BlockSpec((tm, D_MODEL), lambda i: (i, 0)),
                  pl.BlockSpec((D_MODEL, D_MODEL), lambda i: (0, 0),
                               pipeline_mode=pl.Buffered(1)),
                  pl.BlockSpec((tm, D_MODEL), lambda i: (i, 0)),
                  pl.BlockSpec((1, D_MODEL), lambda i: (0, 0))],
        out_specs=pl.BlockSpec((tm, D_MODEL), lambda i: (i, 0)),
        out_shape=jax.ShapeDtypeStruct((t, D_MODEL), F32),
        compiler_params=_params(("parallel",)),
        name="outproj",
    )(y, w, x2, g)


def _layer(x2, batch, seq, layer, g_pre, w_in, lb_logits, conv_w, conv_b, b_ig, b_fg,
           g_norm_a, g_norm_b, w_up_a, w_up_b, w_out, g_post):
    w_in_t = w_in.T
    w_gate = w_in_t[GATE_COL0:GATE_COL0 + N_GATES].astype(BF16)
    bg = jnp.concatenate([b_ig, b_fg]).reshape(N_GATES, 1)

    h, grow = _prenorm(x2, g_pre.reshape(1, D_MODEL), w_gate)
    p = _inproj(h, w_in_t)
    oa = _mixer_a(p, lb_logits, g_norm_a.reshape(1, D_A), layer, batch, seq)
    ob = _mixer_b(p, grow, conv_w, conv_b.reshape(1, 2 * D_QK_B), bg,
                  g_norm_b.reshape(1, D_B), batch, seq)
    y = _merge(oa, ob, w_up_a.astype(BF16), w_up_b.astype(BF16), p)
    return _outproj(y, w_out.astype(BF16), x2, g_post.reshape(1, D_MODEL))


def kernel(x, g_pre, w_in, lb_logits, conv_w, conv_b, b_ig, b_fg, g_norm_a, g_norm_b,
           w_up_a, w_up_b, w_out, g_post):
    batch, seq, _ = x.shape
    depth = g_pre.shape[0]
    x2 = x.reshape(batch * seq, D_MODEL)
    for l in range(depth):
        x2 = _layer(x2, batch, seq, l, g_pre[l], w_in[l], lb_logits, conv_w[l], conv_b[l],
                    b_ig[l], b_fg[l], g_norm_a[l], g_norm_b[l], w_up_a[l], w_up_b[l],
                    w_out[l], g_post[l])
    return x2.reshape(batch, seq, D_MODEL)
```

```python
import functools
import math

import jax
import jax.numpy as jnp
from jax import lax
from jax.experimental import pallas as pl
from jax.experimental.pallas import tpu as pltpu

D_MODEL = 4096
D_A = D_MODEL // 2
HA_DK = 128
H_A = D_A // HA_DK
HA_DV = D_A // H_A
D_B = D_MODEL // 2
H_B = 8
DV_B = D_B // H_B
DK_B = DV_B // 2
D_QK_B = H_B * DK_B
CONV_W = 4
CHUNK = 64
PAIR = 2 * CHUNK
EPS = 1e-6
LANES = 128
SUBLANES = 8

F32 = jnp.float32
BF16 = jnp.bfloat16

OFF_QA, OFF_FA, OFF_IA, OFF_OGA, OFF_ZA = (k * D_A for k in range(5))
OFF_QB = 5 * D_A
OFF_KB = OFF_QB + D_QK_B
OFF_VB = OFF_KB + D_QK_B
OFF_OGB = OFF_VB + D_B
OFF_ZB = OFF_OGB + D_B
OFF_GA = OFF_ZB + D_B
OFF_GB = OFF_GA + D_MODEL
P_COLS = OFF_GB + D_MODEL
GATE_COL0 = OFF_GA
N_GATES = 2 * H_B

VMEM_LIMIT = 60 * 1024 * 1024


def _params(sem):
    return pltpu.CompilerParams(dimension_semantics=sem, vmem_limit_bytes=VMEM_LIMIT)


def _sigmoid(x):
    return 1.0 / (1.0 + jnp.exp(-x))


def _log_sigmoid(x):
    return jnp.minimum(x, 0.0) - jnp.log1p(jnp.exp(-jnp.abs(x)))


def _silu_times_sigmoid(z, og):
    return z / ((1.0 + jnp.exp(-z)) * (1.0 + jnp.exp(-og)))


def _dot(a, b):
    return jnp.dot(a, b, preferred_element_type=F32)


def _dot_nt(a, b):
    return lax.dot_general(a, b, (((1,), (1,)), ((), ())), preferred_element_type=F32)


def _dot_tn(a, b):
    return lax.dot_general(a, b, (((0,), (0,)), ((), ())), preferred_element_type=F32)


def _split_dot(tri, x, left):
    hi = x.astype(BF16)
    lo = (x - hi.astype(F32)).astype(BF16)
    if left:
        return _dot(tri, hi) + _dot(tri, lo)
    return _dot(hi, tri) + _dot(lo, tri)


def _prenorm_kernel(x_ref, g_ref, wg_ref, h_ref, gr_ref):
    x = x_ref[...]
    ms = jnp.mean(x * x, axis=-1, keepdims=True)
    h = ((x * lax.rsqrt(ms + EPS)) * g_ref[...]).astype(BF16)
    h_ref[...] = h
    gr_ref[...] = _dot_nt(wg_ref[...], h)


def _prenorm(x2, g, w_gate, tm=512):
    t = x2.shape[0]
    return pl.pallas_call(
        _prenorm_kernel,
        grid=(t // tm,),
        in_specs=[pl.BlockSpec((tm, D_MODEL), lambda i: (i, 0)),
                  pl.BlockSpec((1, D_MODEL), lambda i: (0, 0)),
                  pl.BlockSpec((N_GATES, D_MODEL), lambda i: (0, 0))],
        out_specs=[pl.BlockSpec((tm, D_MODEL), lambda i: (i, 0)),
                   pl.BlockSpec((N_GATES, tm), lambda i: (0, i))],
        out_shape=[jax.ShapeDtypeStruct((t, D_MODEL), BF16),
                   jax.ShapeDtypeStruct((N_GATES, t), F32)],
        compiler_params=_params(("parallel",)),
        name="prenorm",
    )(x2, g, w_gate)


def _weight_rows(tile, c, *, n_j, tn, rc, first_tail):
    tile = jnp.minimum(tile, n_j - 1)
    skip = jnp.where(tile >= first_tail, N_GATES // SUBLANES, 0)
    return (tile * (tn // SUBLANES) + c * (rc // SUBLANES) + skip) * SUBLANES


def _inproj_kernel(h_ref, wt_hbm, wc_ref, p_ref, wbuf, stage, sem, *, n_i, tn, rows_of, group):
    j = pl.program_id(0)
    i = pl.program_id(1)
    rc = tn // n_i

    @pl.when(j * n_i + i == 0)
    def _():
        def copy(c):
            return pltpu.make_async_copy(wt_hbm.at[pl.ds(rows_of(0, c), rc), :],
                                         stage.at[c % 2], sem.at[c % 2])

        copy(0).start()
        for c in range(n_i):
            if c + 1 < n_i:
                copy(c + 1).start()
            copy(c).wait()
            wbuf[0, c * rc:(c + 1) * rc, :] = stage[c % 2].astype(BF16)

    p_ref[...] = _dot_nt(h_ref[...], wbuf[lax.rem(j, 2)]).astype(BF16)
    sub = pl.ds(pl.multiple_of(lax.rem(i, group) * rc, rc), rc)
    wbuf[lax.rem(j + 1, 2), pl.ds(pl.multiple_of(i * rc, rc), rc), :] = wc_ref[sub, :].astype(BF16)


def _inproj(h, w_in_t, tm=1024, tn=1024, group=4):
    t, k = h.shape
    n_i, n_j = t // tm, P_COLS // tn
    rc = tn // n_i
    rows_of = functools.partial(_weight_rows, n_j=n_j, tn=tn, rc=rc, first_tail=GATE_COL0 // tn)
    return pl.pallas_call(
        functools.partial(_inproj_kernel, n_i=n_i, tn=tn, rows_of=rows_of, group=group),
        grid=(n_j, n_i),
        in_specs=[pl.BlockSpec((tm, k), lambda j, i: (i, 0)),
                  pl.BlockSpec(memory_space=pl.ANY),
                  pl.BlockSpec((pl.Element(group * rc), pl.Element(k)),
                               lambda j, i: (rows_of(j + 1, (i // group) * group), 0))],
        out_specs=pl.BlockSpec((tm, tn), lambda j, i: (i, j)),
        out_shape=jax.ShapeDtypeStruct((t, P_COLS), BF16),
        scratch_shapes=[pltpu.VMEM((2, tn, k), BF16),
                        pltpu.VMEM((2, rc, k), F32),
                        pltpu.SemaphoreType.DMA((2,))],
        compiler_params=_params(("arbitrary", "arbitrary")),
        name="inproj",
    )(h, w_in_t, w_in_t)


def _mixer_a_kernel(q_ref, f_ref, i_ref, og_ref, z_ref, lbl_ref, gn_ref, o_ref, st_ref,
                    *, layer, hpb, ngroup, gsz):
    @pl.when(pl.program_id(2) == 0)
    def _():
        st_ref[...] = jnp.zeros_like(st_ref)

    lg = lbl_ref[...]
    e = jnp.exp(lg - jnp.max(lg, axis=0, keepdims=True))
    lb = jnp.sum(e[:layer + 1], axis=0, keepdims=True) / jnp.sum(e, axis=0, keepdims=True)
    gn = gn_ref[...]

    row = lax.broadcasted_iota(jnp.int32, (CHUNK, CHUNK), 0)
    col = lax.broadcasted_iota(jnp.int32, (CHUNK, CHUNK), 1)
    causal = row >= col
    tri = jnp.where(causal, 1.0, 0.0).astype(BF16)
    mid = CHUNK // 2 - 1
    heads = [slice(h * HA_DK, (h + 1) * HA_DK) for h in range(hpb)]

    def group(u, carry):
        rows = [pl.ds(pl.multiple_of((u * gsz + c) * CHUNK, CHUNK), CHUNK) for c in range(gsz)]
        fs = [lb + (1.0 - lb) * _sigmoid(f_ref[r, :].astype(F32)) for r in rows]
        bs = [_split_dot(tri, jnp.log(f), left=True) for f in fs]
        q_rel, k_rel, q_in, k_out, eg = [], [], [], [], []
        for c in range(gsz):
            b = bs[c]
            b_mid = b[mid:mid + 1, :]
            g = b[CHUNK - 1:CHUNK, :]
            e_rel = jnp.exp(b - b_mid)
            qr = q_ref[rows[c], :].astype(F32) * e_rel
            kr = (1.0 - fs[c]) / e_rel
            q_in.append((qr * jnp.exp(b_mid)).astype(BF16))
            k_out.append((kr * jnp.exp(g - b_mid)).astype(BF16))
            q_rel.append(qr.astype(BF16))
            k_rel.append(kr.astype(BF16))
            eg.append(jnp.exp(g))
        vs = [i_ref[r, :] for r in rows]
        scores = [[jnp.where(causal, _dot_nt(q_rel[c][:, sl], k_rel[c][:, sl]), 0.0).astype(BF16)
                   for sl in heads] for c in range(gsz)]
        upd = [[_dot_tn(vs[c][:, sl], k_out[c][:, sl]) for sl in heads] for c in range(gsz)]
        st = [st_ref[h] for h in range(hpb)]
        for c in range(gsz):
            gate = _silu_times_sigmoid(z_ref[rows[c], :].astype(F32),
                                       og_ref[rows[c], :].astype(F32))
            for h, sl in enumerate(heads):
                o = _dot(scores[c][h], vs[c][:, sl]) + _dot_nt(q_in[c][:, sl], st[h].astype(BF16))
                st[h] = st[h] * eg[c][:, sl] + upd[c][h]
                ms = jnp.mean(o * o, axis=-1, keepdims=True)
                res = (o * lax.rsqrt(ms + EPS)) * gn[:, sl] * gate[:, sl]
                o_ref[rows[c], sl] = res.astype(BF16)
        for h in range(hpb):
            st_ref[h] = st[h]
        return carry

    lax.fori_loop(0, ngroup, group, 0)


def _mixer_a(p, lb_logits, gn_a, layer, batch, seq, tb=1024, hpb=4, gsz=8):
    t = p.shape[0]
    cw = hpb * HA_DK
    nb = seq // tb
    ng = D_A // cw

    def spec(off):
        return pl.BlockSpec((tb, cw), lambda b, g, n, o=off // cw: (b * nb + n, o + g))

    return pl.pallas_call(
        functools.partial(_mixer_a_kernel, layer=layer, hpb=hpb,
                          ngroup=tb // (CHUNK * gsz), gsz=gsz),
        grid=(batch, ng, nb),
        in_specs=[spec(OFF_QA), spec(OFF_FA), spec(OFF_IA), spec(OFF_OGA), spec(OFF_ZA),
                  pl.BlockSpec((lb_logits.shape[0], cw), lambda b, g, n: (0, g)),
                  pl.BlockSpec((1, cw), lambda b, g, n: (0, g))],
        out_specs=pl.BlockSpec((tb, cw), lambda b, g, n: (b * nb + n, g)),
        out_shape=jax.ShapeDtypeStruct((t, D_A), BF16),
        scratch_shapes=[pltpu.VMEM((hpb, HA_DV, HA_DK), F32)],
        compiler_params=_params(("parallel", "parallel", "arbitrary")),
        name="mixer_a",
    )(p, p, p, p, p, lb_logits, gn_a)


def _mixer_b_kernel(qk_ref, v_ref, og_ref, z_ref, gr_ref, grn_ref, cw_ref, cb_ref, bg_ref, gn_ref,
                    o_ref, prev_ref, c_ref, m_ref, a_scr, colf_scr, dec_scr, *, npair):

    r_sh = lax.broadcasted_iota(jnp.int32, (CHUNK, PAIR), 0)
    c_sh = lax.broadcasted_iota(jnp.int32, (CHUNK, PAIR), 1)
    shift = jnp.concatenate(
        [jnp.where(c_sh == r_sh + (CHUNK - s), 1.0, 0.0) for s in range(1, CONV_W)],
        axis=0).astype(BF16)
    row = lax.broadcasted_iota(jnp.int32, (CHUNK, CHUNK), 0)
    col = lax.broadcasted_iota(jnp.int32, (CHUNK, CHUNK), 1)
    causal = row >= col
    def lane_of(i):
        return jnp.where(i >= CHUNK, i - CHUNK, i)

    r_p = lax.broadcasted_iota(jnp.int32, (PAIR, PAIR), 0)
    c_p = lax.broadcasted_iota(jnp.int32, (PAIR, PAIR), 1)
    tri2 = jnp.where(r_p <= c_p, jnp.where(c_p - r_p <= lane_of(c_p), 1.0, 0.0),
                     0.0).astype(BF16)
    lane = lax.broadcasted_iota(jnp.int32, (H_B, PAIR), 1)
    second = lane >= CHUNK
    lane_in_chunk = lane_of(lane)
    ones_v = jnp.ones((CHUNK, LANES), BF16)
    cw = cw_ref[...]
    cb = cb_ref[...]
    gn = gn_ref[...]
    bg = bg_ref[...]
    log_kscale = -0.5 * math.log(DK_B)

    def gates(g, m_prev):
        g = g + bg
        b = _split_dot(tri2, _log_sigmoid(g[H_B:]), left=False)
        a = g[:H_B] - b
        cm = a
        for sft in (1, 2, 4, 8, 16, 32):
            cm = jnp.maximum(cm, jnp.where(lane_in_chunk >= sft,
                                           pltpu.roll(cm, sft, axis=1), -jnp.inf))
        big_m_a_end = jnp.maximum(cm[:, CHUNK - 1:CHUNK], m_prev)
        m_mid = b[:, CHUNK - 1:CHUNK] + big_m_a_end
        m_start = jnp.where(second, m_mid, m_prev)
        big_m = jnp.maximum(cm, m_start)
        big_m_b_end = big_m[:, PAIR - 1:PAIR]
        big_m_end = jnp.where(second, big_m_b_end, big_m_a_end)
        inter = jnp.exp(m_start - big_m)
        enm = jnp.exp(-(b + big_m))
        a_k = a + log_kscale
        w = jnp.exp(a_k - big_m_end)
        colf = jnp.concatenate([big_m, inter, enm, w], axis=0).T
        return (a_k, colf, jnp.exp(m_prev - big_m_a_end), jnp.exp(m_mid - big_m_b_end),
                b[:, PAIR - 1:PAIR] + big_m_b_end)

    def stash(gq):
        a_scr[...] = gq[0]
        colf_scr[...] = gq[1]
        dec_scr[0] = gq[2]
        dec_scr[1] = gq[3]
        m_ref[...] = gq[4]

    def pair(u, carry):
        base = pl.multiple_of(u * PAIR, PAIR)
        rows = [pl.ds(pl.multiple_of(u * PAIR + c * CHUNK, CHUNK), CHUNK) for c in range(2)]

        cur = qk_ref[pl.ds(base, PAIR), :]
        windows = [jnp.concatenate([prev_ref[...], cur[:CHUNK]], axis=0), cur]
        prev_ref[...] = cur[CHUNK:]
        qk = []
        for c in range(2):
            qk.append([])
            for j in range(2 * H_B):
                ls = slice(j * DK_B, (j + 1) * DK_B)
                sh = _dot(shift, windows[c][:, ls])
                acc = cb[:, ls] + cw[CONV_W - 1:CONV_W, ls] * windows[c][CHUNK:, ls].astype(F32)
                for s in range(1, CONV_W):
                    acc = acc + cw[CONV_W - 1 - s:CONV_W - s, ls] * sh[(s - 1) * CHUNK:s * CHUNK]
                act = acc / (1.0 + jnp.exp(-acc))
                qk[c].append(act.astype(BF16))

        a = a_scr[...]
        colf = colf_scr[...]
        decay = [dec_scr[0], dec_scr[1]]
        nxt_base = pl.multiple_of(jnp.minimum(u + 1, npair - 1) * PAIR, PAIR)
        g_next = jnp.where(u + 1 < npair, gr_ref[:, pl.ds(nxt_base, PAIR)], grn_ref[:, :PAIR])
        nxt = gates(g_next, m_ref[...])

        def colv(c, kind, h, width):
            x = colf[c * CHUNK:(c + 1) * CHUNK, kind * H_B + h:kind * H_B + h + 1]
            return jnp.broadcast_to(x, (CHUNK, width))

        for c in range(2):
            dws, qis, kws = [], [], []
            for h in range(H_B):
                qh, kh = qk[c][h], qk[c][H_B + h]
                a_row = a[h:h + 1, c * CHUNK:(c + 1) * CHUNK]
                d = jnp.where(causal, jnp.exp(a_row - colv(c, 0, h, CHUNK)), 0.0)
                dws.append((d * _dot_nt(qh, kh)).astype(BF16))
                qis.append((qh.astype(F32) * colv(c, 1, h, DK_B)).astype(BF16))
                kws.append((kh.astype(F32) * colv(c, 3, h, DK_B)).astype(BF16))
            for h in range(H_B):
                sl = slice(h * DV_B, (h + 1) * DV_B)
                vaug = jnp.concatenate([v_ref[rows[c], sl], ones_v], axis=1)
                cst = c_ref[h]
                acc2 = _dot(dws[h], vaug) + _dot(qis[h], cst.astype(BF16))
                dec = jnp.broadcast_to(decay[c][h:h + 1, :], (DK_B, DV_B + LANES))
                c_ref[h] = dec * cst + _dot_tn(kws[h], vaug)
                rden = 1.0 / jnp.maximum(jnp.abs(acc2[:, DV_B:]), colv(c, 2, h, LANES))
                hh = acc2[:, :DV_B] * jnp.concatenate([rden, rden], axis=1)
                ms = jnp.mean(hh * hh, axis=-1, keepdims=True)
                gate = _silu_times_sigmoid(z_ref[rows[c], sl].astype(F32),
                                           og_ref[rows[c], sl].astype(F32))
                res = (hh * lax.rsqrt(ms + EPS)) * gn[:, sl] * gate
                o_ref[rows[c], sl] = res.astype(BF16)
        stash(nxt)
        return carry

    @pl.when(pl.program_id(1) == 0)
    def _():
        prev_ref[...] = jnp.zeros_like(prev_ref)
        c_ref[...] = jnp.zeros_like(c_ref)
        stash(gates(gr_ref[:, :PAIR], jnp.zeros((H_B, 1), F32)))

    lax.fori_loop(0, npair, pair, 0)


def _mixer_b(p, grow, conv_w, conv_b, bg, gn_b, batch, seq, tb=1024):
    t = p.shape[0]
    nb = seq // tb

    def spec(width, off):
        return pl.BlockSpec((tb, width), lambda b, n, o=off // width: (b * nb + n, o))

    def full(shape):
        return pl.BlockSpec(shape, lambda b, n: (0,) * len(shape))

    return pl.pallas_call(
        functools.partial(_mixer_b_kernel, npair=tb // PAIR),
        grid=(batch, nb),
        in_specs=[spec(2 * D_QK_B, OFF_QB), spec(D_B, OFF_VB),
                  spec(D_B, OFF_OGB), spec(D_B, OFF_ZB),
                  pl.BlockSpec((N_GATES, tb), lambda b, n: (0, b * nb + n)),
                  pl.BlockSpec((N_GATES, tb),
                               lambda b, n: (0, jnp.minimum(b * nb + n + 1, batch * nb - 1))),
                  full((CONV_W, 2 * D_QK_B)), full((1, 2 * D_QK_B)),
                  full((N_GATES, 1)), full((1, D_B))],
        out_specs=pl.BlockSpec((tb, D_B), lambda b, n: (b * nb + n, 0)),
        out_shape=jax.ShapeDtypeStruct((t, D_B), BF16),
        scratch_shapes=[pltpu.VMEM((CHUNK, 2 * D_QK_B), BF16),
                        pltpu.VMEM((H_B, DK_B, DV_B + LANES), F32),
                        pltpu.VMEM((H_B, 1), F32),
                        pltpu.VMEM((H_B, PAIR), F32),
                        pltpu.VMEM((PAIR, 4 * H_B), F32),
                        pltpu.VMEM((2, H_B, 1), F32)],
        compiler_params=_params(("parallel", "arbitrary")),
        name="mixer_b",
    )(p, p, p, p, grow, grow, conv_w, conv_b, bg, gn_b)


def _merge_kernel(oa_ref, ob_ref, wa_ref, wb_ref, ga_ref, gb_ref, y_ref):
    ya = _dot(oa_ref[...], wa_ref[...])
    yb = _dot(ob_ref[...], wb_ref[...])
    y = _sigmoid(ga_ref[...].astype(F32)) * ya + _sigmoid(gb_ref[...].astype(F32)) * yb
    y_ref[...] = y.astype(BF16)


def _merge(oa, ob, wa, wb, p, tm=1024, tn=1024):
    t = oa.shape[0]
    return pl.pallas_call(
        _merge_kernel,
        grid=(t // tm, D_MODEL // tn),
        in_specs=[pl.BlockSpec((tm, D_A), lambda i, j: (i, 0)),
                  pl.BlockSpec((tm, D_B), lambda i, j: (i, 0)),
                  pl.BlockSpec((D_A, tn), lambda i, j: (0, j)),
                  pl.BlockSpec((D_B, tn), lambda i, j: (0, j)),
                  pl.BlockSpec((tm, tn), lambda i, j: (i, OFF_GA // tn + j)),
                  pl.BlockSpec((tm, tn), lambda i, j: (i, OFF_GB // tn + j))],
        out_specs=pl.BlockSpec((tm, tn), lambda i, j: (i, j)),
        out_shape=jax.ShapeDtypeStruct((t, D_MODEL), BF16),
        compiler_params=_params(("parallel", "arbitrary")),
        name="merge",
    )(oa, ob, wa, wb, p, p)


def _outproj_kernel(y_ref, w_ref, x_ref, g_ref, o_ref):
    o = _dot(y_ref[...], w_ref[...])
    ms = jnp.mean(o * o, axis=-1, keepdims=True)
    o_ref[...] = x_ref[...] + (o * lax.rsqrt(ms + EPS)) * g_ref[...]


def _outproj(y, w, x2, g, tm=256):
    t = y.shape[0]
    return pl.pallas_call(
        _outproj_kernel,
        grid=(t // tm,),
        in_specs=[pl.BlockSpec((tm, D_MODEL), lambda i: (i, 0)),
                  pl.BlockSpec((D_MODEL, D_MODEL), lambda i: (0, 0),
                               pipeline_mode=pl.Buffered(1)),
                  pl.BlockSpec((tm, D_MODEL), lambda i: (i, 0)),
                  pl.BlockSpec((1, D_MODEL), lambda i: (0, 0))],
        out_specs=pl.BlockSpec((tm, D_MODEL), lambda i: (i, 0)),
        out_shape=jax.ShapeDtypeStruct((t, D_MODEL), F32),
        compiler_params=_params(("parallel",)),
        name="outproj",
    )(y, w, x2, g)


def _layer(x2, batch, seq, layer, g_pre, w_in, lb_logits, conv_w, conv_b, b_ig, b_fg,
           g_norm_a, g_norm_b, w_up_a, w_up_b, w_out, g_post):
    w_in_t = w_in.T
    w_gate = w_in_t[GATE_COL0:GATE_COL0 + N_GATES].astype(BF16)
    bg = jnp.concatenate([b_ig, b_fg]).reshape(N_GATES, 1)

    h, grow = _prenorm(x2, g_pre.reshape(1, D_MODEL), w_gate)
    p = _inproj(h, w_in_t)
    oa = _mixer_a(p, lb_logits, g_norm_a.reshape(1, D_A), layer, batch, seq)
    ob = _mixer_b(p, grow, conv_w, conv_b.reshape(1, 2 * D_QK_B), bg,
                  g_norm_b.reshape(1, D_B), batch, seq)
    y = _merge(oa, ob, w_up_a.astype(BF16), w_up_b.astype(BF16), p)
    return _outproj(y, w_out.astype(BF16), x2, g_post.reshape(1, D_MODEL))


def kernel(x, g_pre, w_in, lb_logits, conv_w, conv_b, b_ig, b_fg, g_norm_a, g_norm_b,
           w_up_a, w_up_b, w_out, g_post):
    batch, seq, _ = x.shape
    depth = g_pre.shape[0]
    x2 = x.reshape(batch * seq, D_MODEL)
    for l in range(depth):
        x2 = _layer(x2, batch, seq, l, g_pre[l], w_in[l], lb_logits, conv_w[l], conv_b[l],
                    b_ig[l], b_fg[l], g_norm_a[l], g_norm_b[l], w_up_a[l], w_up_b[l],
                    w_out[l], g_post[l])
    return x2.reshape(batch, seq, D_MODEL)
```

```python
import functools
import math

import jax
import jax.numpy as jnp
from jax import lax
from jax.experimental import pallas as pl
from jax.experimental.pallas import tpu as pltpu

D_MODEL = 4096
D_A = D_MODEL // 2
HA_DK = 128
H_A = D_A // HA_DK
HA_DV = D_A // H_A
D_B = D_MODEL // 2
H_B = 8
DV_B = D_B // H_B
DK_B = DV_B // 2
D_QK_B = H_B * DK_B
CONV_W = 4
CHUNK = 64
PAIR = 2 * CHUNK
EPS = 1e-6
LANES = 128
SUBLANES = 8

F32 = jnp.float32
BF16 = jnp.bfloat16

OFF_QA, OFF_FA, OFF_IA, OFF_OGA, OFF_ZA = (k * D_A for k in range(5))
OFF_QB = 5 * D_A
OFF_KB = OFF_QB + D_QK_B
OFF_VB = OFF_KB + D_QK_B
OFF_OGB = OFF_VB + D_B
OFF_ZB = OFF_OGB + D_B
OFF_GA = OFF_ZB + D_B
OFF_GB = OFF_GA + D_MODEL
P_COLS = OFF_GB + D_MODEL
GATE_COL0 = OFF_GA
N_GATES = 2 * H_B

VMEM_LIMIT = 60 * 1024 * 1024


def _params(sem):
    return pltpu.CompilerParams(dimension_semantics=sem, vmem_limit_bytes=VMEM_LIMIT)


def _sigmoid(x):
    return 1.0 / (1.0 + jnp.exp(-x))


def _log_sigmoid(x):
    return jnp.minimum(x, 0.0) - jnp.log1p(jnp.exp(-jnp.abs(x)))


def _silu_times_sigmoid(z, og):
    return (0.25 * z) * ((1.0 + jnp.tanh(0.5 * z)) * (1.0 + jnp.tanh(0.5 * og)))


def _dot(a, b):
    return jnp.dot(a, b, preferred_element_type=F32)


def _dot_nt(a, b):
    return lax.dot_general(a, b, (((1,), (1,)), ((), ())), preferred_element_type=F32)


def _dot_tn(a, b):
    return lax.dot_general(a, b, (((0,), (0,)), ((), ())), preferred_element_type=F32)


def _split_dot(tri, x, left):
    hi = x.astype(BF16)
    lo = (x - hi.astype(F32)).astype(BF16)
    if left:
        return _dot(tri, hi) + _dot(tri, lo)
    return _dot(hi, tri) + _dot(lo, tri)


def _prenorm_kernel(x_ref, g_ref, wg_ref, h_ref, gr_ref):
    x = x_ref[...]
    ms = jnp.mean(x * x, axis=-1, keepdims=True)
    h = ((x * lax.rsqrt(ms + EPS)) * g_ref[...]).astype(BF16)
    h_ref[...] = h
    gr_ref[...] = _dot_nt(wg_ref[...], h)


def _prenorm(x2, g, w_gate, tm=512):
    t = x2.shape[0]
    return pl.pallas_call(
        _prenorm_kernel,
        grid=(t // tm,),
        in_specs=[pl.BlockSpec((tm, D_MODEL), lambda i: (i, 0)),
                  pl.BlockSpec((1, D_MODEL), lambda i: (0, 0)),
                  pl.BlockSpec((N_GATES, D_MODEL), lambda i: (0, 0))],
        out_specs=[pl.BlockSpec((tm, D_MODEL), lambda i: (i, 0)),
                   pl.BlockSpec((N_GATES, tm), lambda i: (0, i))],
        out_shape=[jax.ShapeDtypeStruct((t, D_MODEL), BF16),
                   jax.ShapeDtypeStruct((N_GATES, t), F32)],
        compiler_params=_params(("parallel",)),
        name="prenorm",
    )(x2, g, w_gate)


def _weight_rows(tile, c, *, n_j, tn, rc, first_tail):
    tile = jnp.minimum(tile, n_j - 1)
    skip = jnp.where(tile >= first_tail, N_GATES // SUBLANES, 0)
    return (tile * (tn // SUBLANES) + c * (rc // SUBLANES) + skip) * SUBLANES


def _inproj_kernel(h_ref, wt_hbm, wc_ref, p_ref, wbuf, stage, sem, *, n_i, tn, rows_of, group):
    j = pl.program_id(0)
    i = pl.program_id(1)
    rc = tn // n_i

    @pl.when(j * n_i + i == 0)
    def _():
        def copy(c):
            return pltpu.make_async_copy(wt_hbm.at[pl.ds(rows_of(0, c), rc), :],
                                         stage.at[c % 2], sem.at[c % 2])

        copy(0).start()
        for c in range(n_i):
            if c + 1 < n_i:
                copy(c + 1).start()
            copy(c).wait()
            wbuf[0, c * rc:(c + 1) * rc, :] = stage[c % 2].astype(BF16)

    p_ref[...] = _dot_nt(h_ref[...], wbuf[lax.rem(j, 2)]).astype(BF16)
    sub = pl.ds(pl.multiple_of(lax.rem(i, group) * rc, rc), rc)
    wbuf[lax.rem(j + 1, 2), pl.ds(pl.multiple_of(i * rc, rc), rc), :] = wc_ref[sub, :].astype(BF16)


def _inproj(h, w_in_t, tm=1024, tn=1024, group=4):
    t, k = h.shape
    n_i, n_j = t // tm, P_COLS // tn
    rc = tn // n_i
    rows_of = functools.partial(_weight_rows, n_j=n_j, tn=tn, rc=rc, first_tail=GATE_COL0 // tn)
    return pl.pallas_call(
        functools.partial(_inproj_kernel, n_i=n_i, tn=tn, rows_of=rows_of, group=group),
        grid=(n_j, n_i),
        in_specs=[pl.BlockSpec((tm, k), lambda j, i: (i, 0)),
                  pl.BlockSpec(memory_space=pl.ANY),
                  pl.BlockSpec((pl.Element(group * rc), pl.Element(k)),
                               lambda j, i: (rows_of(j + 1, (i // group) * group), 0))],
        out_specs=pl.BlockSpec((tm, tn), lambda j, i: (i, j)),
        out_shape=jax.ShapeDtypeStruct((t, P_COLS), BF16),
        scratch_shapes=[pltpu.VMEM((2, tn, k), BF16),
                        pltpu.VMEM((2, rc, k), F32),
                        pltpu.SemaphoreType.DMA((2,))],
        compiler_params=_params(("arbitrary", "arbitrary")),
        name="inproj",
    )(h, w_in_t, w_in_t)


def _mixer_a_kernel(q_ref, f_ref, i_ref, og_ref, z_ref, lbl_ref, gn_ref, o_ref, st_ref,
                    *, layer, hpb, ngroup, gsz):
    @pl.when(pl.program_id(2) == 0)
    def _():
        st_ref[...] = jnp.zeros_like(st_ref)

    lg = lbl_ref[...]
    e = jnp.exp(lg - jnp.max(lg, axis=0, keepdims=True))
    lb = jnp.sum(e[:layer + 1], axis=0, keepdims=True) / jnp.sum(e, axis=0, keepdims=True)
    f_mid = 0.5 * (1.0 + lb)
    f_half = 0.5 * (1.0 - lb)
    gn = gn_ref[...]

    row = lax.broadcasted_iota(jnp.int32, (CHUNK, CHUNK), 0)
    col = lax.broadcasted_iota(jnp.int32, (CHUNK, CHUNK), 1)
    causal = row >= col
    tri = jnp.where(causal, 1.0, 0.0).astype(BF16)
    mid = CHUNK // 2 - 1
    heads = [slice(h * HA_DK, (h + 1) * HA_DK) for h in range(hpb)]

    def group(u, carry):
        rows = [pl.ds(pl.multiple_of((u * gsz + c) * CHUNK, CHUNK), CHUNK) for c in range(gsz)]
        fs = [f_mid + f_half * jnp.tanh(0.5 * f_ref[r, :].astype(F32)) for r in rows]
        bs = [_split_dot(tri, jnp.log(f), left=True) for f in fs]
        q_rel, k_rel, q_in, k_out, eg = [], [], [], [], []
        for c in range(gsz):
            b = bs[c]
            b_mid = b[mid:mid + 1, :]
            g = b[CHUNK - 1:CHUNK, :]
            e_rel = jnp.exp(b - b_mid)
            qr = q_ref[rows[c], :].astype(F32) * e_rel
            kr = (1.0 - fs[c]) / e_rel
            q_in.append((qr * jnp.exp(b_mid)).astype(BF16))
            k_out.append((kr * jnp.exp(g - b_mid)).astype(BF16))
            q_rel.append(qr.astype(BF16))
            k_rel.append(kr.astype(BF16))
            eg.append(jnp.exp(g))
        vs = [i_ref[r, :] for r in rows]
        scores = [[jnp.where(causal, _dot_nt(q_rel[c][:, sl], k_rel[c][:, sl]), 0.0).astype(BF16)
                   for sl in heads] for c in range(gsz)]
        upd = [[_dot_tn(vs[c][:, sl], k_out[c][:, sl]) for sl in heads] for c in range(gsz)]
        st = [st_ref[h] for h in range(hpb)]
        for c in range(gsz):
            gate = _silu_times_sigmoid(z_ref[rows[c], :].astype(F32),
                                       og_ref[rows[c], :].astype(F32))
            for h, sl in enumerate(heads):
                o = _dot(scores[c][h], vs[c][:, sl]) + _dot_nt(q_in[c][:, sl], st[h].astype(BF16))
                st[h] = st[h] * eg[c][:, sl] + upd[c][h]
                ms = jnp.mean(o * o, axis=-1, keepdims=True)
                res = (o * lax.rsqrt(ms + EPS)) * gn[:, sl] * gate[:, sl]
                o_ref[rows[c], sl] = res.astype(BF16)
        for h in range(hpb):
            st_ref[h] = st[h]
        return carry

    lax.fori_loop(0, ngroup, group, 0)


def _mixer_a(p, lb_logits, gn_a, layer, batch, seq, tb=1024, hpb=8, gsz=8):
    t = p.shape[0]
    cw = hpb * HA_DK
    nb = seq // tb
    ng = D_A // cw

    def spec(off):
        return pl.BlockSpec((tb, cw), lambda b, g, n, o=off // cw: (b * nb + n, o + g))

    return pl.pallas_call(
        functools.partial(_mixer_a_kernel, layer=layer, hpb=hpb,
                          ngroup=tb // (CHUNK * gsz), gsz=gsz),
        grid=(batch, ng, nb),
        in_specs=[spec(OFF_QA), spec(OFF_FA), spec(OFF_IA), spec(OFF_OGA), spec(OFF_ZA),
                  pl.BlockSpec((lb_logits.shape[0], cw), lambda b, g, n: (0, g)),
                  pl.BlockSpec((1, cw), lambda b, g, n: (0, g))],
        out_specs=pl.BlockSpec((tb, cw), lambda b, g, n: (b * nb + n, g)),
        out_shape=jax.ShapeDtypeStruct((t, D_A), BF16),
        scratch_shapes=[pltpu.VMEM((hpb, HA_DV, HA_DK), F32)],
        compiler_params=_params(("parallel", "parallel", "arbitrary")),
        name="mixer_a",
    )(p, p, p, p, p, lb_logits, gn_a)


def _mixer_b_kernel(qk_ref, v_ref, og_ref, z_ref, gr_ref, grn_ref, cw_ref, cb_ref, bg_ref, gn_ref,
                    o_ref, prev_ref, c_ref, m_ref, a_scr, colf_scr, dec_scr, *, npair):

    r_sh = lax.broadcasted_iota(jnp.int32, (CHUNK, PAIR), 0)
    c_sh = lax.broadcasted_iota(jnp.int32, (CHUNK, PAIR), 1)
    shift = jnp.concatenate(
        [jnp.where(c_sh == r_sh + (CHUNK - s), 1.0, 0.0) for s in range(1, CONV_W)],
        axis=0).astype(BF16)
    row = lax.broadcasted_iota(jnp.int32, (CHUNK, CHUNK), 0)
    col = lax.broadcasted_iota(jnp.int32, (CHUNK, CHUNK), 1)
    causal = row >= col
    def lane_of(i):
        return jnp.where(i >= CHUNK, i - CHUNK, i)

    r_p = lax.broadcasted_iota(jnp.int32, (PAIR, PAIR), 0)
    c_p = lax.broadcasted_iota(jnp.int32, (PAIR, PAIR), 1)
    tri2 = jnp.where(r_p <= c_p, jnp.where(c_p - r_p <= lane_of(c_p), 1.0, 0.0),
                     0.0).astype(BF16)
    lane = lax.broadcasted_iota(jnp.int32, (H_B, PAIR), 1)
    second = lane >= CHUNK
    lane_in_chunk = lane_of(lane)
    ones_v = jnp.ones((CHUNK, LANES), BF16)
    cw = cw_ref[...]
    cb = cb_ref[...]
    gn = gn_ref[...]
    bg = bg_ref[...]
    log_kscale = -0.5 * math.log(DK_B)

    def gates(g, m_prev):
        g = g + bg
        b = _split_dot(tri2, _log_sigmoid(g[H_B:]), left=False)
        a = g[:H_B] - b
        cm = a
        for sft in (1, 2, 4, 8, 16, 32):
            cm = jnp.maximum(cm, jnp.where(lane_in_chunk >= sft,
                                           pltpu.roll(cm, sft, axis=1), -jnp.inf))
        big_m_a_end = jnp.maximum(cm[:, CHUNK - 1:CHUNK], m_prev)
        m_mid = b[:, CHUNK - 1:CHUNK] + big_m_a_end
        m_start = jnp.where(second, m_mid, m_prev)
        big_m = jnp.maximum(cm, m_start)
        big_m_b_end = big_m[:, PAIR - 1:PAIR]
        big_m_end = jnp.where(second, big_m_b_end, big_m_a_end)
        inter = jnp.exp(m_start - big_m)
        enm = jnp.exp(-(b + big_m))
        a_k = a + log_kscale
        w = jnp.exp(a_k - big_m_end)
        colf = jnp.concatenate([big_m, inter, enm, w], axis=0).T
        return (a_k, colf, jnp.exp(m_prev - big_m_a_end), jnp.exp(m_mid - big_m_b_end),
                b[:, PAIR - 1:PAIR] + big_m_b_end)

    def stash(gq):
        a_scr[...] = gq[0]
        colf_scr[...] = gq[1]
        dec_scr[0] = gq[2]
        dec_scr[1] = gq[3]
        m_ref[...] = gq[4]

    def pair(u, carry):
        base = pl.multiple_of(u * PAIR, PAIR)
        rows = [pl.ds(pl.multiple_of(u * PAIR + c * CHUNK, CHUNK), CHUNK) for c in range(2)]

        cur = qk_ref[pl.ds(base, PAIR), :]
        windows = [jnp.concatenate([prev_ref[...], cur[:CHUNK]], axis=0), cur]
        prev_ref[...] = cur[CHUNK:]
        qk = []
        for c in range(2):
            qk.append([])
            for j in range(2 * H_B):
                ls = slice(j * DK_B, (j + 1) * DK_B)
                sh = _dot(shift, windows[c][:, ls])
                acc = cb[:, ls] + cw[CONV_W - 1:CONV_W, ls] * windows[c][CHUNK:, ls].astype(F32)
                for s in range(1, CONV_W):
                    acc = acc + cw[CONV_W - 1 - s:CONV_W - s, ls] * sh[(s - 1) * CHUNK:s * CHUNK]
                act = acc / (1.0 + jnp.exp(-acc))
                qk[c].append(act.astype(BF16))

        a = a_scr[...]
        colf = colf_scr[...]
        decay = [dec_scr[0], dec_scr[1]]
        nxt_base = pl.multiple_of(jnp.minimum(u + 1, npair - 1) * PAIR, PAIR)
        g_next = jnp.where(u + 1 < npair, gr_ref[:, pl.ds(nxt_base, PAIR)], grn_ref[:, :PAIR])
        nxt = gates(g_next, m_ref[...])

        def colv(c, kind, h, width):
            x = colf[c * CHUNK:(c + 1) * CHUNK, kind * H_B + h:kind * H_B + h + 1]
            return jnp.broadcast_to(x, (CHUNK, width))

        for c in range(2):
            dws, qis, kws = [], [], []
            for h in range(H_B):
                qh, kh = qk[c][h], qk[c][H_B + h]
                a_row = a[h:h + 1, c * CHUNK:(c + 1) * CHUNK]
                d = jnp.where(causal, jnp.exp(a_row - colv(c, 0, h, CHUNK)), 0.0)
                dws.append((d * _dot_nt(qh, kh)).astype(BF16))
                qis.append((qh.astype(F32) * colv(c, 1, h, DK_B)).astype(BF16))
                kws.append((kh.astype(F32) * colv(c, 3, h, DK_B)).astype(BF16))
            for h in range(H_B):
                sl = slice(h * DV_B, (h + 1) * DV_B)
                vaug = jnp.concatenate([v_ref[rows[c], sl], ones_v], axis=1)
                cst = c_ref[h]
                acc2 = _dot(dws[h], vaug) + _dot(qis[h], cst.astype(BF16))
                dec = jnp.broadcast_to(decay[c][h:h + 1, :], (DK_B, DV_B + LANES))
                c_ref[h] = dec * cst + _dot_tn(kws[h], vaug)
                rden = 1.0 / jnp.maximum(jnp.abs(acc2[:, DV_B:]), colv(c, 2, h, LANES))
                hh = acc2[:, :DV_B] * jnp.concatenate([rden, rden], axis=1)
                ms = jnp.mean(hh * hh, axis=-1, keepdims=True)
                gate = _silu_times_sigmoid(z_ref[rows[c], sl].astype(F32),
                                           og_ref[rows[c], sl].astype(F32))
                res = (hh * lax.rsqrt(ms + EPS)) * gn[:, sl] * gate
                o_ref[rows[c], sl] = res.astype(BF16)
        stash(nxt)
        return carry

    @pl.when(pl.program_id(1) == 0)
    def _():
        prev_ref[...] = jnp.zeros_like(prev_ref)
        c_ref[...] = jnp.zeros_like(c_ref)
        stash(gates(gr_ref[:, :PAIR], jnp.zeros((H_B, 1), F32)))

    lax.fori_loop(0, npair, pair, 0)


def _mixer_b(p, grow, conv_w, conv_b, bg, gn_b, batch, seq, tb=1024):
    t = p.shape[0]
    nb = seq // tb

    def spec(width, off):
        return pl.BlockSpec((tb, width), lambda b, n, o=off // width: (b * nb + n, o))

    def full(shape):
        return pl.BlockSpec(shape, lambda b, n: (0,) * len(shape))

    return pl.pallas_call(
        functools.partial(_mixer_b_kernel, npair=tb // PAIR),
        grid=(batch, nb),
        in_specs=[spec(2 * D_QK_B, OFF_QB), spec(D_B, OFF_VB),
                  spec(D_B, OFF_OGB), spec(D_B, OFF_ZB),
                  pl.BlockSpec((N_GATES, tb), lambda b, n: (0, b * nb + n)),
                  pl.BlockSpec((N_GATES, tb),
                               lambda b, n: (0, jnp.minimum(b * nb + n + 1, batch * nb - 1))),
                  full((CONV_W, 2 * D_QK_B)), full((1, 2 * D_QK_B)),
                  full((N_GATES, 1)), full((1, D_B))],
        out_specs=pl.BlockSpec((tb, D_B), lambda b, n: (b * nb + n, 0)),
        out_shape=jax.ShapeDtypeStruct((t, D_B), BF16),
        scratch_shapes=[pltpu.VMEM((CHUNK, 2 * D_QK_B), BF16),
                        pltpu.VMEM((H_B, DK_B, DV_B + LANES), F32),
                        pltpu.VMEM((H_B, 1), F32),
                        pltpu.VMEM((H_B, PAIR), F32),
                        pltpu.VMEM((PAIR, 4 * H_B), F32),
                        pltpu.VMEM((2, H_B, 1), F32)],
        compiler_params=_params(("parallel", "arbitrary")),
        name="mixer_b",
    )(p, p, p, p, grow, grow, conv_w, conv_b, bg, gn_b)


def _merge_kernel(oa_ref, ob_ref, wa_ref, wb_ref, ga_ref, gb_ref, y_ref):
    ya = _dot(oa_ref[...], wa_ref[...])
    yb = _dot(ob_ref[...], wb_ref[...])
    y = _sigmoid(ga_ref[...].astype(F32)) * ya + _sigmoid(gb_ref[...].astype(F32)) * yb
    y_ref[...] = y.astype(BF16)


def _merge(oa, ob, wa, wb, p, tm=1024, tn=1024):
    t = oa.shape[0]
    return pl.pallas_call(
        _merge_kernel,
        grid=(t // tm, D_MODEL // tn),
        in_specs=[pl.BlockSpec((tm, D_A), lambda i, j: (i, 0)),
                  pl.BlockSpec((tm, D_B), lambda i, j: (i, 0)),
                  pl.BlockSpec((D_A, tn), lambda i, j: (0, j)),
                  pl.BlockSpec((D_B, tn), lambda i, j: (0, j)),
                  pl.BlockSpec((tm, tn), lambda i, j: (i, OFF_GA // tn + j)),
                  pl.BlockSpec((tm, tn), lambda i, j: (i, OFF_GB // tn + j))],
        out_specs=pl.BlockSpec((tm, tn), lambda i, j: (i, j)),
        out_shape=jax.ShapeDtypeStruct((t, D_MODEL), BF16),
        compiler_params=_params(("parallel", "arbitrary")),
        name="merge",
    )(oa, ob, wa, wb, p, p)


def _outproj_kernel(y_ref, w_ref, x_ref, g_ref, o_ref):
    o = _dot(y_ref[...], w_ref[...])
    ms = jnp.mean(o * o, axis=-1, keepdims=True)
    o_ref[...] = x_ref[...] + (o * lax.rsqrt(ms + EPS)) * g_ref[...]


def _outproj(y, w, x2, g, tm=256):
    t = y.shape[0]
    return pl.pallas_call(
        _outproj_kernel,
        grid=(t // tm,),
        in_specs=[pl.BlockSpec((tm, D_MODEL), lambda i: (i, 0)),
                  pl.BlockSpec((D_MODEL, D_MODEL), lambda i: (0, 0),
                               pipeline_mode=pl.Buffered(1)),
                  pl.BlockSpec((tm, D_MODEL), lambda i: (i, 0)),
                  pl.BlockSpec((1, D_MODEL), lambda i: (0, 0))],
        out_specs=pl.BlockSpec((tm, D_MODEL), lambda i: (i, 0)),
        out_shape=jax.ShapeDtypeStruct((t, D_MODEL), F32),
        compiler_params=_params(("parallel",)),
        name="outproj",
    )(y, w, x2, g)


def _layer(x2, batch, seq, layer, g_pre, w_in, lb_logits, conv_w, conv_b, b_ig, b_fg,
           g_norm_a, g_norm_b, w_up_a, w_up_b, w_out, g_post):
    w_in_t = w_in.T
    w_gate = w_in_t[GATE_COL0:GATE_COL0 + N_GATES].astype(BF16)
    bg = jnp.concatenate([b_ig, b_fg]).reshape(N_GATES, 1)

    h, grow = _prenorm(x2, g_pre.reshape(1, D_MODEL), w_gate)
    p = _inproj(h, w_in_t)
    oa = _mixer_a(p, lb_logits, g_norm_a.reshape(1, D_A), layer, batch, seq)
    ob = _mixer_b(p, grow, conv_w, conv_b.reshape(1, 2 * D_QK_B), bg,
                  g_norm_b.reshape(1, D_B), batch, seq)
    y = _merge(oa, ob, w_up_a.astype(BF16), w_up_b.astype(BF16), p)
    return _outproj(y, w_out.astype(BF16), x2, g_post.reshape(1, D_MODEL))


def kernel(x, g_pre, w_in, lb_logits, conv_w, conv_b, b_ig, b_fg, g_norm_a, g_norm_b,
           w_up_a, w_up_b, w_out, g_post):
    batch, seq, _ = x.shape
    depth = g_pre.shape[0]
    x2 = x.reshape(batch * seq, D_MODEL)
    for l in range(depth):
        x2 = _layer(x2, batch, seq, l, g_pre[l], w_in[l], lb_logits, conv_w[l], conv_b[l],
                    b_ig[l], b_fg[l], g_norm_a[l], g_norm_b[l], w_up_a[l], w_up_b[l],
                    w_out[l], g_post[l])
    return x2.reshape(batch, seq, D_MODEL)
```

```python
import functools
import math

import jax
import jax.numpy as jnp
from jax import lax
from jax.experimental import pallas as pl
from jax.experimental.pallas import tpu as pltpu

D_MODEL = 4096
D_A = D_MODEL // 2
HA_DK = 128
H_A = D_A // HA_DK
HA_DV = D_A // H_A
D_B = D_MODEL // 2
H_B = 8
DV_B = D_B // H_B
DK_B = DV_B // 2
D_QK_B = H_B * DK_B
CONV_W = 4
CHUNK = 64
PAIR = 2 * CHUNK
EPS = 1e-6
LANES = 128
SUBLANES = 8

F32 = jnp.float32
BF16 = jnp.bfloat16

OFF_QA, OFF_FA, OFF_IA, OFF_OGA, OFF_ZA = (k * D_A for k in range(5))
OFF_QB = 5 * D_A
OFF_KB = OFF_QB + D_QK_B
OFF_VB = OFF_KB + D_QK_B
OFF_OGB = OFF_VB + D_B
OFF_ZB = OFF_OGB + D_B
OFF_GA = OFF_ZB + D_B
OFF_GB = OFF_GA + D_MODEL
P_COLS = OFF_GB + D_MODEL
GATE_COL0 = OFF_GA
N_GATES = 2 * H_B

VMEM_LIMIT = 60 * 1024 * 1024


def _params(sem):
    return pltpu.CompilerParams(dimension_semantics=sem, vmem_limit_bytes=VMEM_LIMIT)


def _sigmoid(x):
    return 0.5 + 0.5 * jnp.tanh(0.5 * x)


def _log_sigmoid(x):
    return jnp.minimum(x, 0.0) - jnp.log1p(jnp.exp(-jnp.abs(x)))


def _silu_times_sigmoid(z, og):
    return (0.25 * z) * ((1.0 + jnp.tanh(0.5 * z)) * (1.0 + jnp.tanh(0.5 * og)))


def _dot(a, b):
    return jnp.dot(a, b, preferred_element_type=F32)


def _dot_nt(a, b):
    return lax.dot_general(a, b, (((1,), (1,)), ((), ())), preferred_element_type=F32)


def _dot_tn(a, b):
    return lax.dot_general(a, b, (((0,), (0,)), ((), ())), preferred_element_type=F32)


def _split_dot(tri, x, left):
    hi = x.astype(BF16)
    lo = (x - hi.astype(F32)).astype(BF16)
    if left:
        return _dot(tri, hi) + _dot(tri, lo)
    return _dot(hi, tri) + _dot(lo, tri)


def _prenorm_kernel(x_ref, g_ref, wg_ref, h_ref, gr_ref):
    x = x_ref[...]
    ms = jnp.mean(x * x, axis=-1, keepdims=True)
    h = ((x * lax.rsqrt(ms + EPS)) * g_ref[...]).astype(BF16)
    h_ref[...] = h
    gr_ref[...] = _dot_nt(wg_ref[...], h)


def _prenorm(x2, g, w_gate, tm=512):
    t = x2.shape[0]
    return pl.pallas_call(
        _prenorm_kernel,
        grid=(t // tm,),
        in_specs=[pl.BlockSpec((tm, D_MODEL), lambda i: (i, 0)),
                  pl.BlockSpec((1, D_MODEL), lambda i: (0, 0)),
                  pl.BlockSpec((N_GATES, D_MODEL), lambda i: (0, 0))],
        out_specs=[pl.BlockSpec((tm, D_MODEL), lambda i: (i, 0)),
                   pl.BlockSpec((N_GATES, tm), lambda i: (0, i))],
        out_shape=[jax.ShapeDtypeStruct((t, D_MODEL), BF16),
                   jax.ShapeDtypeStruct((N_GATES, t), F32)],
        compiler_params=_params(("parallel",)),
        name="prenorm",
    )(x2, g, w_gate)


def _weight_rows(tile, c, *, n_j, tn, rc, first_tail):
    tile = jnp.minimum(tile, n_j - 1)
    skip = jnp.where(tile >= first_tail, N_GATES // SUBLANES, 0)
    return (tile * (tn // SUBLANES) + c * (rc // SUBLANES) + skip) * SUBLANES


def _inproj_kernel(h_ref, wt_hbm, wc_ref, p_ref, wbuf, stage, sem, *, n_i, tn, rows_of, group):
    j = pl.program_id(0)
    i = pl.program_id(1)
    rc = tn // n_i

    @pl.when(j * n_i + i == 0)
    def _():
        def copy(c):
            return pltpu.make_async_copy(wt_hbm.at[pl.ds(rows_of(0, c), rc), :],
                                         stage.at[c % 2], sem.at[c % 2])

        copy(0).start()
        for c in range(n_i):
            if c + 1 < n_i:
                copy(c + 1).start()
            copy(c).wait()
            wbuf[0, c * rc:(c + 1) * rc, :] = stage[c % 2].astype(BF16)

    p_ref[...] = _dot_nt(h_ref[...], wbuf[lax.rem(j, 2)]).astype(BF16)
    sub = pl.ds(pl.multiple_of(lax.rem(i, group) * rc, rc), rc)
    wbuf[lax.rem(j + 1, 2), pl.ds(pl.multiple_of(i * rc, rc), rc), :] = wc_ref[sub, :].astype(BF16)


def _inproj(h, w_in_t, tm=1024, tn=1024, group=4):
    t, k = h.shape
    n_i, n_j = t // tm, P_COLS // tn
    rc = tn // n_i
    rows_of = functools.partial(_weight_rows, n_j=n_j, tn=tn, rc=rc, first_tail=GATE_COL0 // tn)
    return pl.pallas_call(
        functools.partial(_inproj_kernel, n_i=n_i, tn=tn, rows_of=rows_of, group=group),
        grid=(n_j, n_i),
        in_specs=[pl.BlockSpec((tm, k), lambda j, i: (i, 0)),
                  pl.BlockSpec(memory_space=pl.ANY),
                  pl.BlockSpec((pl.Element(group * rc), pl.Element(k)),
                               lambda j, i: (rows_of(j + 1, (i // group) * group), 0))],
        out_specs=pl.BlockSpec((tm, tn), lambda j, i: (i, j)),
        out_shape=jax.ShapeDtypeStruct((t, P_COLS), BF16),
        scratch_shapes=[pltpu.VMEM((2, tn, k), BF16),
                        pltpu.VMEM((2, rc, k), F32),
                        pltpu.SemaphoreType.DMA((2,))],
        compiler_params=_params(("arbitrary", "arbitrary")),
        name="inproj",
    )(h, w_in_t, w_in_t)


def _mixer_a_kernel(q_ref, f_ref, i_ref, og_ref, z_ref, lbl_ref, gn_ref, o_ref, st_ref,
                    *, layer, hpb, ngroup, gsz):
    @pl.when(pl.program_id(2) == 0)
    def _():
        st_ref[...] = jnp.zeros_like(st_ref)

    lg = lbl_ref[...]
    e = jnp.exp(lg - jnp.max(lg, axis=0, keepdims=True))
    lb = jnp.sum(e[:layer + 1], axis=0, keepdims=True) / jnp.sum(e, axis=0, keepdims=True)
    f_mid = 0.5 * (1.0 + lb)
    f_half = 0.5 * (1.0 - lb)
    gn = gn_ref[...]

    row = lax.broadcasted_iota(jnp.int32, (CHUNK, CHUNK), 0)
    col = lax.broadcasted_iota(jnp.int32, (CHUNK, CHUNK), 1)
    causal = row >= col
    tri = jnp.where(causal, 1.0, 0.0).astype(BF16)
    mid = CHUNK // 2 - 1
    heads = [slice(h * HA_DK, (h + 1) * HA_DK) for h in range(hpb)]

    def group(u, carry):
        rows = [pl.ds(pl.multiple_of((u * gsz + c) * CHUNK, CHUNK), CHUNK) for c in range(gsz)]
        fs = [f_mid + f_half * jnp.tanh(0.5 * f_ref[r, :].astype(F32)) for r in rows]
        bs = [_split_dot(tri, jnp.log(f), left=True) for f in fs]
        q_rel, k_rel, q_in, k_out, eg = [], [], [], [], []
        for c in range(gsz):
            b = bs[c]
            b_mid = b[mid:mid + 1, :]
            g = b[CHUNK - 1:CHUNK, :]
            e_rel = jnp.exp(b - b_mid)
            qr = q_ref[rows[c], :].astype(F32) * e_rel
            kr = (1.0 - fs[c]) / e_rel
            q_in.append((qr * jnp.exp(b_mid)).astype(BF16))
            k_out.append((kr * jnp.exp(g - b_mid)).astype(BF16))
            q_rel.append(qr.astype(BF16))
            k_rel.append(kr.astype(BF16))
            eg.append(jnp.exp(g))
        vs = [i_ref[r, :] for r in rows]
        scores = [[jnp.where(causal, _dot_nt(q_rel[c][:, sl], k_rel[c][:, sl]), 0.0).astype(BF16)
                   for sl in heads] for c in range(gsz)]
        upd = [[_dot_tn(vs[c][:, sl], k_out[c][:, sl]) for sl in heads] for c in range(gsz)]
        st = [st_ref[h] for h in range(hpb)]
        for c in range(gsz):
            gate = _silu_times_sigmoid(z_ref[rows[c], :].astype(F32),
                                       og_ref[rows[c], :].astype(F32))
            for h, sl in enumerate(heads):
                o = _dot(scores[c][h], vs[c][:, sl]) + _dot_nt(q_in[c][:, sl], st[h].astype(BF16))
                st[h] = st[h] * eg[c][:, sl] + upd[c][h]
                ms = jnp.mean(o * o, axis=-1, keepdims=True)
                res = (o * lax.rsqrt(ms + EPS)) * gn[:, sl] * gate[:, sl]
                o_ref[rows[c], sl] = res.astype(BF16)
        for h in range(hpb):
            st_ref[h] = st[h]
        return carry

    lax.fori_loop(0, ngroup, group, 0)


def _mixer_a(p, lb_logits, gn_a, layer, batch, seq, tb=1024, hpb=8, gsz=8):
    t = p.shape[0]
    cw = hpb * HA_DK
    nb = seq // tb
    ng = D_A // cw

    def spec(off):
        return pl.BlockSpec((tb, cw), lambda b, g, n, o=off // cw: (b * nb + n, o + g))

    return pl.pallas_call(
        functools.partial(_mixer_a_kernel, layer=layer, hpb=hpb,
                          ngroup=tb // (CHUNK * gsz), gsz=gsz),
        grid=(batch, ng, nb),
        in_specs=[spec(OFF_QA), spec(OFF_FA), spec(OFF_IA), spec(OFF_OGA), spec(OFF_ZA),
                  pl.BlockSpec((lb_logits.shape[0], cw), lambda b, g, n: (0, g)),
                  pl.BlockSpec((1, cw), lambda b, g, n: (0, g))],
        out_specs=pl.BlockSpec((tb, cw), lambda b, g, n: (b * nb + n, g)),
        out_shape=jax.ShapeDtypeStruct((t, D_A), BF16),
        scratch_shapes=[pltpu.VMEM((hpb, HA_DV, HA_DK), F32)],
        compiler_params=_params(("parallel", "parallel", "arbitrary")),
        name="mixer_a",
    )(p, p, p, p, p, lb_logits, gn_a)


def _mixer_b_kernel(qk_ref, v_ref, og_ref, z_ref, gr_ref, grn_ref, cw_ref, cb_ref, bg_ref, gn_ref,
                    o_ref, prev_ref, c_ref, m_ref, a_scr, colf_scr, dec_scr, *, npair):

    r_sh = lax.broadcasted_iota(jnp.int32, (CHUNK, PAIR), 0)
    c_sh = lax.broadcasted_iota(jnp.int32, (CHUNK, PAIR), 1)
    shift = jnp.concatenate(
        [jnp.where(c_sh == r_sh + (CHUNK - s), 1.0, 0.0) for s in range(1, CONV_W)],
        axis=0).astype(BF16)
    row = lax.broadcasted_iota(jnp.int32, (CHUNK, CHUNK), 0)
    col = lax.broadcasted_iota(jnp.int32, (CHUNK, CHUNK), 1)
    causal = row >= col
    def lane_of(i):
        return jnp.where(i >= CHUNK, i - CHUNK, i)

    r_p = lax.broadcasted_iota(jnp.int32, (PAIR, PAIR), 0)
    c_p = lax.broadcasted_iota(jnp.int32, (PAIR, PAIR), 1)
    tri2 = jnp.where(r_p <= c_p, jnp.where(c_p - r_p <= lane_of(c_p), 1.0, 0.0),
                     0.0).astype(BF16)
    lane = lax.broadcasted_iota(jnp.int32, (H_B, PAIR), 1)
    second = lane >= CHUNK
    lane_in_chunk = lane_of(lane)
    ones_v = jnp.ones((CHUNK, LANES), BF16)
    cw = cw_ref[...]
    cb = cb_ref[...]
    gn = gn_ref[...]
    bg = bg_ref[...]
    log_kscale = -0.5 * math.log(DK_B)

    def gates(g, m_prev):
        g = g + bg
        b = _split_dot(tri2, _log_sigmoid(g[H_B:]), left=False)
        a = g[:H_B] - b
        cm = a
        for sft in (1, 2, 4, 8, 16, 32):
            cm = jnp.maximum(cm, jnp.where(lane_in_chunk >= sft,
                                           pltpu.roll(cm, sft, axis=1), -jnp.inf))
        big_m_a_end = jnp.maximum(cm[:, CHUNK - 1:CHUNK], m_prev)
        m_mid = b[:, CHUNK - 1:CHUNK] + big_m_a_end
        m_start = jnp.where(second, m_mid, m_prev)
        big_m = jnp.maximum(cm, m_start)
        big_m_b_end = big_m[:, PAIR - 1:PAIR]
        big_m_end = jnp.where(second, big_m_b_end, big_m_a_end)
        inter = jnp.exp(m_start - big_m)
        enm = jnp.exp(-(b + big_m))
        a_k = a + log_kscale
        w = jnp.exp(a_k - big_m_end)
        colf = jnp.concatenate([big_m, inter, enm, w], axis=0).T
        return (a_k, colf, jnp.exp(m_prev - big_m_a_end), jnp.exp(m_mid - big_m_b_end),
                b[:, PAIR - 1:PAIR] + big_m_b_end)

    def stash(gq):
        a_scr[...] = gq[0]
        colf_scr[...] = gq[1]
        dec_scr[0] = gq[2]
        dec_scr[1] = gq[3]
        m_ref[...] = gq[4]

    def pair(u, carry):
        base = pl.multiple_of(u * PAIR, PAIR)
        rows = [pl.ds(pl.multiple_of(u * PAIR + c * CHUNK, CHUNK), CHUNK) for c in range(2)]

        cur = qk_ref[pl.ds(base, PAIR), :]
        windows = [jnp.concatenate([prev_ref[...], cur[:CHUNK]], axis=0), cur]
        prev_ref[...] = cur[CHUNK:]
        qk = []
        for c in range(2):
            qk.append([])
            for j in range(2 * H_B):
                ls = slice(j * DK_B, (j + 1) * DK_B)
                sh = _dot(shift, windows[c][:, ls])
                acc = cb[:, ls] + cw[CONV_W - 1:CONV_W, ls] * windows[c][CHUNK:, ls].astype(F32)
                for s in range(1, CONV_W):
                    acc = acc + cw[CONV_W - 1 - s:CONV_W - s, ls] * sh[(s - 1) * CHUNK:s * CHUNK]
                act = acc / (1.0 + jnp.exp(-acc))
                qk[c].append(act.astype(BF16))

        a = a_scr[...]
        colf = colf_scr[...]
        decay = [dec_scr[0], dec_scr[1]]
        nxt_base = pl.multiple_of(jnp.minimum(u + 1, npair - 1) * PAIR, PAIR)
        g_next = jnp.where(u + 1 < npair, gr_ref[:, pl.ds(nxt_base, PAIR)], grn_ref[:, :PAIR])
        nxt = gates(g_next, m_ref[...])

        def colv(c, kind, h, width):
            x = colf[c * CHUNK:(c + 1) * CHUNK, kind * H_B + h:kind * H_B + h + 1]
            return jnp.broadcast_to(x, (CHUNK, width))

        for c in range(2):
            dws, qis, kws = [], [], []
            for h in range(H_B):
                qh, kh = qk[c][h], qk[c][H_B + h]
                a_row = a[h:h + 1, c * CHUNK:(c + 1) * CHUNK]
                d = jnp.where(causal, jnp.exp(a_row - colv(c, 0, h, CHUNK)), 0.0)
                dws.append((d * _dot_nt(qh, kh)).astype(BF16))
                qis.append((qh.astype(F32) * colv(c, 1, h, DK_B)).astype(BF16))
                kws.append((kh.astype(F32) * colv(c, 3, h, DK_B)).astype(BF16))
            for h in range(H_B):
                sl = slice(h * DV_B, (h + 1) * DV_B)
                vaug = jnp.concatenate([v_ref[rows[c], sl], ones_v], axis=1)
                cst = c_ref[h]
                acc2 = _dot(dws[h], vaug) + _dot(qis[h], cst.astype(BF16))
                dec = jnp.broadcast_to(decay[c][h:h + 1, :], (DK_B, DV_B + LANES))
                c_ref[h] = dec * cst + _dot_tn(kws[h], vaug)
                rden = 1.0 / jnp.maximum(jnp.abs(acc2[:, DV_B:]), colv(c, 2, h, LANES))
                hh = acc2[:, :DV_B] * jnp.concatenate([rden, rden], axis=1)
                ms = jnp.mean(hh * hh, axis=-1, keepdims=True)
                gate = _silu_times_sigmoid(z_ref[rows[c], sl].astype(F32),
                                           og_ref[rows[c], sl].astype(F32))
                res = (hh * lax.rsqrt(ms + EPS)) * gn[:, sl] * gate
                o_ref[rows[c], sl] = res.astype(BF16)
        stash(nxt)
        return carry

    @pl.when(pl.program_id(1) == 0)
    def _():
        prev_ref[...] = jnp.zeros_like(prev_ref)
        c_ref[...] = jnp.zeros_like(c_ref)
        stash(gates(gr_ref[:, :PAIR], jnp.zeros((H_B, 1), F32)))

    lax.fori_loop(0, npair, pair, 0)


def _mixer_b(p, grow, conv_w, conv_b, bg, gn_b, batch, seq, tb=1024):
    t = p.shape[0]
    nb = seq // tb

    def spec(width, off):
        return pl.BlockSpec((tb, width), lambda b, n, o=off // width: (b * nb + n, o))

    def full(shape):
        return pl.BlockSpec(shape, lambda b, n: (0,) * len(shape))

    return pl.pallas_call(
        functools.partial(_mixer_b_kernel, npair=tb // PAIR),
        grid=(batch, nb),
        in_specs=[spec(2 * D_QK_B, OFF_QB), spec(D_B, OFF_VB),
                  spec(D_B, OFF_OGB), spec(D_B, OFF_ZB),
                  pl.BlockSpec((N_GATES, tb), lambda b, n: (0, b * nb + n)),
                  pl.BlockSpec((N_GATES, tb),
                               lambda b, n: (0, jnp.minimum(b * nb + n + 1, batch * nb - 1))),
                  full((CONV_W, 2 * D_QK_B)), full((1, 2 * D_QK_B)),
                  full((N_GATES, 1)), full((1, D_B))],
        out_specs=pl.BlockSpec((tb, D_B), lambda b, n: (b * nb + n, 0)),
        out_shape=jax.ShapeDtypeStruct((t, D_B), BF16),
        scratch_shapes=[pltpu.VMEM((CHUNK, 2 * D_QK_B), BF16),
                        pltpu.VMEM((H_B, DK_B, DV_B + LANES), F32),
                        pltpu.VMEM((H_B, 1), F32),
                        pltpu.VMEM((H_B, PAIR), F32),
                        pltpu.VMEM((PAIR, 4 * H_B), F32),
                        pltpu.VMEM((2, H_B, 1), F32)],
        compiler_params=_params(("parallel", "arbitrary")),
        name="mixer_b",
    )(p, p, p, p, grow, grow, conv_w, conv_b, bg, gn_b)


def _merge_kernel(oa_ref, ob_ref, wa_ref, wb_ref, ga_ref, gb_ref, y_ref):
    ya = _dot(oa_ref[...], wa_ref[...])
    yb = _dot(ob_ref[...], wb_ref[...])
    y = _sigmoid(ga_ref[...].astype(F32)) * ya + _sigmoid(gb_ref[...].astype(F32)) * yb
    y_ref[...] = y.astype(BF16)


def _merge(oa, ob, wa, wb, p, tm=1024, tn=1024):
    t = oa.shape[0]
    return pl.pallas_call(
        _merge_kernel,
        grid=(t // tm, D_MODEL // tn),
        in_specs=[pl.BlockSpec((tm, D_A), lambda i, j: (i, 0)),
                  pl.BlockSpec((tm, D_B), lambda i, j: (i, 0)),
                  pl.BlockSpec((D_A, tn), lambda i, j: (0, j)),
                  pl.BlockSpec((D_B, tn), lambda i, j: (0, j)),
                  pl.BlockSpec((tm, tn), lambda i, j: (i, OFF_GA // tn + j)),
                  pl.BlockSpec((tm, tn), lambda i, j: (i, OFF_GB // tn + j))],
        out_specs=pl.BlockSpec((tm, tn), lambda i, j: (i, j)),
        out_shape=jax.ShapeDtypeStruct((t, D_MODEL), BF16),
        compiler_params=_params(("parallel", "arbitrary")),
        name="merge",
    )(oa, ob, wa, wb, p, p)


def _outproj_kernel(y_ref, w_ref, x_ref, g_ref, o_ref):
    o = _dot(y_ref[...], w_ref[...])
    ms = jnp.mean(o * o, axis=-1, keepdims=True)
    o_ref[...] = x_ref[...] + (o * lax.rsqrt(ms + EPS)) * g_ref[...]


def _outproj(y, w, x2, g, tm=256):
    t = y.shape[0]
    return pl.pallas_call(
        _outproj_kernel,
        grid=(t // tm,),
        in_specs=[pl.BlockSpec((tm, D_MODEL), lambda i: (i, 0)),
                  pl.BlockSpec((D_MODEL, D_MODEL), lambda i: (0, 0),
                               pipeline_mode=pl.Buffered(1)),
                  pl.BlockSpec((tm, D_MODEL), lambda i: (i, 0)),
                  pl.BlockSpec((1, D_MODEL), lambda i: (0, 0))],
        out_specs=pl.BlockSpec((tm, D_MODEL), lambda i: (i, 0)),
        out_shape=jax.ShapeDtypeStruct((t, D_MODEL), F32),
        compiler_params=_params(("parallel",)),
        name="outproj",
    )(y, w, x2, g)


def _layer(x2, batch, seq, layer, g_pre, w_in, lb_logits, conv_w, conv_b, b_ig, b_fg,
           g_norm_a, g_norm_b, w_up_a, w_up_b, w_out, g_post):
    w_in_t = w_in.T
    w_gate = w_in_t[GATE_COL0:GATE_COL0 + N_GATES].astype(BF16)
    bg = jnp.concatenate([b_ig, b_fg]).reshape(N_GATES, 1)

    h, grow = _prenorm(x2, g_pre.reshape(1, D_MODEL), w_gate)
    p = _inproj(h, w_in_t)
    oa = _mixer_a(p, lb_logits, g_norm_a.reshape(1, D_A), layer, batch, seq)
    ob = _mixer_b(p, grow, conv_w, conv_b.reshape(1, 2 * D_QK_B), bg,
                  g_norm_b.reshape(1, D_B), batch, seq)
    y = _merge(oa, ob, w_up_a.astype(BF16), w_up_b.astype(BF16), p)
    return _outproj(y, w_out.astype(BF16), x2, g_post.reshape(1, D_MODEL))


def kernel(x, g_pre, w_in, lb_logits, conv_w, conv_b, b_ig, b_fg, g_norm_a, g_norm_b,
           w_up_a, w_up_b, w_out, g_post):
    batch, seq, _ = x.shape
    depth = g_pre.shape[0]
    x2 = x.reshape(batch * seq, D_MODEL)
    for l in range(depth):
        x2 = _layer(x2, batch, seq, l, g_pre[l], w_in[l], lb_logits, conv_w[l], conv_b[l],
                    b_ig[l], b_fg[l], g_norm_a[l], g_norm_b[l], w_up_a[l], w_up_b[l],
                    w_out[l], g_post[l])
    return x2.reshape(batch, seq, D_MODEL)
```
